```python
import math
import jax
import jax.numpy as jnp
from jax import lax
import numpy as np

D_MODEL = 2048
BATCH = 2
SEQ = 16384
DEPTH = 2

HEAD_DIM = 64
NSA_HEADS = 16
NSA_KV_HEADS = 2
NSA_HPG = NSA_HEADS // NSA_KV_HEADS
CMP_LEN = 32
CMP_STRIDE = 16
CMP_HIDDEN = 128
SLC_BLOCK = 64
N_SELECT = 16
NSA_WINDOW = 512
FORCE_BONUS = 1e4
SWA_HEADS = 16
SWA_KV_HEADS = 2
SWA_HPG = SWA_HEADS // SWA_KV_HEADS
SWA_WINDOW = 128
REL_BUCKETS = 32
REL_EXACT = 16
REL_MAX_DIST = 2048
PEER_HEADS = 8
PEER_NKEYS = 128
PEER_EXPERTS = PEER_NKEYS * PEER_NKEYS
PEER_QDIM = 256
PEER_TOPK = 16
PEER_CHUNK = 128

Q_BLOCK = 128
NEG = -1e30
RMS_EPS = 1e-6
NSA_Q_W = NSA_HEADS * HEAD_DIM
NSA_KV_W = NSA_KV_HEADS * HEAD_DIM
NSA_GATE_W = NSA_HEADS * 3
SWA_Q_W = SWA_HEADS * HEAD_DIM
SWA_KV_W = SWA_KV_HEADS * HEAD_DIM
MIX_WIDTH = NSA_Q_W + SWA_Q_W
PROJ_SPLITS = (NSA_Q_W, NSA_KV_W, NSA_KV_W, NSA_KV_W, NSA_KV_W, NSA_KV_W, NSA_KV_W, NSA_GATE_W, SWA_Q_W, SWA_KV_W, SWA_KV_W)
PROJ_WIDTH = sum(PROJ_SPLITS)
SLC_OFFSETS = (-1, 0, 1, 2, 3)
SLC_WEIGHTS = (1.0, 2.0, 2.0, 2.0, 1.0)

kernel_name = 'nsa_swa_sink_peer_hybrid'


def rmsnorm(x, g):
    xf = x.astype(jnp.float32)
    y = xf * lax.rsqrt(jnp.mean(xf * xf, axis=-1, keepdims=True) + RMS_EPS)
    return (y * g.astype(jnp.float32)).astype(x.dtype)


def rel_bucket(dist):
    d = jnp.maximum(dist, 0)
    n_log = REL_BUCKETS - REL_EXACT
    large = REL_EXACT + (jnp.log(jnp.maximum(d, 1).astype(jnp.float32) / REL_EXACT)
                         / math.log(REL_MAX_DIST / REL_EXACT) * n_log).astype(jnp.int32)
    large = jnp.minimum(large, REL_BUCKETS - 1)
    return jnp.where(d < REL_EXACT, d, large)


def masked_softmax(logits, valid):
    p = jax.nn.softmax(jnp.where(valid, logits, NEG), axis=-1)
    return jnp.where(valid, p, 0.0)


def compress(k, pos, w1, w2):
    B, S, G, Dh = k.shape
    chunks = k.reshape(B, S // CMP_STRIDE, CMP_STRIDE, G, Dh)
    blocks = jnp.concatenate([chunks[:, :-1], chunks[:, 1:]], axis=2)
    blocks = blocks + pos[None, None, :, None, :]
    flat = jnp.moveaxis(blocks, 3, 2).reshape(B, blocks.shape[1], G, CMP_LEN * Dh)
    return jax.nn.gelu(flat @ w1) @ w2


def hybrid_mixer(xn, w_in, cmp_pos_k, cmp_w1_k, cmp_w2_k, cmp_pos_v, cmp_w1_v, cmp_w2_v,
                 sinks, rel_bias, w_out):
    B, S, _ = xn.shape
    f32 = jnp.float32
    GA, GB = NSA_KV_HEADS, SWA_KV_HEADS
    NS = S // SLC_BLOCK
    n_sel = min(N_SELECT, NS)
    scale = HEAD_DIM ** -0.5
    offs = [int(c) for c in np.cumsum(PROJ_SPLITS)[:-1]]
    (q_a, kc, vc, ks, vs, kw, vw, gate_a, q_b, k_b, v_b) = jnp.split(xn @ w_in, offs, axis=-1)

    q_a = q_a.reshape(B, S, GA, NSA_HPG, HEAD_DIM)
    q_b = q_b.reshape(B, S, GB, SWA_HPG, HEAD_DIM)
    gate_a = jax.nn.sigmoid(gate_a).reshape(B, S, GA, NSA_HPG, 3)
    kc_c = compress(kc.reshape(B, S, GA, HEAD_DIM), cmp_pos_k, cmp_w1_k, cmp_w2_k)
    vc_c = compress(vc.reshape(B, S, GA, HEAD_DIM), cmp_pos_v, cmp_w1_v, cmp_w2_v)
    n_cmp = kc_c.shape[1]
    cmp_end = jnp.arange(n_cmp) * CMP_STRIDE + CMP_LEN - 1
    ks_blk = ks.reshape(B, NS, SLC_BLOCK, GA, HEAD_DIM).transpose(0, 3, 1, 2, 4)
    vs_blk = vs.reshape(B, NS, SLC_BLOCK, GA, HEAD_DIM).transpose(0, 3, 1, 2, 4)
    pad_a = ((0, 0), (NSA_WINDOW, 0), (0, 0), (0, 0))
    kw_pad = jnp.pad(kw.reshape(B, S, GA, HEAD_DIM), pad_a)
    vw_pad = jnp.pad(vw.reshape(B, S, GA, HEAD_DIM), pad_a)
    pad_b = ((0, 0), (SWA_WINDOW, 0), (0, 0), (0, 0))
    kb_pad = jnp.pad(k_b.reshape(B, S, GB, HEAD_DIM), pad_b)
    vb_pad = jnp.pad(v_b.reshape(B, S, GB, HEAD_DIM), pad_b)

    bias_a = rel_bias[:, :NSA_HEADS].reshape(REL_BUCKETS, GA, NSA_HPG).astype(f32)
    bias_b = rel_bias[:, NSA_HEADS:].reshape(REL_BUCKETS, GB, SWA_HPG).astype(f32)
    sink = sinks.astype(f32).reshape(1, GB, SWA_HPG, 1, 1)
    b_idx = jnp.arange(B)[:, None, None, None]
    g_idx = jnp.arange(GA)[None, :, None, None]
    blk_ids = jnp.arange(NS)
    slc_tok = jnp.arange(SLC_BLOCK)

    def token_bias(table, dist):
        return table[rel_bucket(dist)].transpose(2, 3, 0, 1)

    def query_block(i):
        q0 = i * Q_BLOCK
        pos_q = q0 + jnp.arange(Q_BLOCK)
        qa = lax.dynamic_slice_in_dim(q_a, q0, Q_BLOCK, axis=1)
        dt = qa.dtype

        dist_c = pos_q[:, None] - cmp_end[None, :]
        lg_c = jnp.einsum('bqghd,bcgd->bghqc', qa, kc_c, preferred_element_type=f32) * scale
        p_cmp = masked_softmax(lg_c + token_bias(bias_a, dist_c), dist_c >= 0)
        o_cmp = jnp.einsum('bghqc,bcgd->bqghd', p_cmp.astype(dt), vc_c)

        imp = jnp.pad(p_cmp.sum(axis=2), ((0, 0), (0, 0), (0, 0), (1, 1)))
        imp_s = jnp.zeros(imp.shape[:-1] + (NS,), f32)
        for r, w in zip(SLC_OFFSETS, SLC_WEIGHTS):
            imp_s = imp_s + w * imp[..., r + 1: r + 2 + 4 * (NS - 1): 4]
        cb = pos_q[:, None] // SLC_BLOCK
        j = blk_ids[None, :]
        forced = (j == 0) | (j == cb) | (j == cb - 1)
        score = jnp.where(j > cb, NEG, imp_s + jnp.where(forced, FORCE_BONUS, 0.0))
        _, sel = lax.top_k(score, n_sel)
        ks_g = ks_blk[b_idx, g_idx, sel].reshape(B, GA, Q_BLOCK, n_sel * SLC_BLOCK, HEAD_DIM)
        vs_g = vs_blk[b_idx, g_idx, sel].reshape(B, GA, Q_BLOCK, n_sel * SLC_BLOCK, HEAD_DIM)
        kpos = (sel[..., None] * SLC_BLOCK + slc_tok).reshape(B, GA, Q_BLOCK, n_sel * SLC_BLOCK)
        dist_s = pos_q[:, None] - kpos
        bias_s = jnp.moveaxis(bias_a[rel_bucket(dist_s), g_idx], -1, 2)
        lg_s = jnp.einsum('bqghd,bgqkd->bghqk', qa, ks_g, preferred_element_type=f32) * scale
        p_s = masked_softmax(lg_s + bias_s, (dist_s >= 0)[:, :, None])
        o_slc = jnp.einsum('bghqk,bgqkd->bqghd', p_s.astype(dt), vs_g)

        La = Q_BLOCK + NSA_WINDOW
        kwin = lax.dynamic_slice_in_dim(kw_pad, q0, La, axis=1)
        vwin = lax.dynamic_slice_in_dim(vw_pad, q0, La, axis=1)
        kpos_w = q0 - NSA_WINDOW + jnp.arange(La)
        dist_w = pos_q[:, None] - kpos_w[None, :]
        valid_w = (dist_w >= 0) & (dist_w < NSA_WINDOW) & (kpos_w[None, :] >= 0)
        lg_w = jnp.einsum('bqghd,blgd->bghql', qa, kwin, preferred_element_type=f32) * scale
        p_w = masked_softmax(lg_w + token_bias(bias_a, dist_w), valid_w)
        o_win = jnp.einsum('bghql,blgd->bqghd', p_w.astype(dt), vwin)

        g = lax.dynamic_slice_in_dim(gate_a, q0, Q_BLOCK, axis=1)
        o_a = g[..., 0:1] * o_cmp + g[..., 1:2] * o_slc + g[..., 2:3] * o_win

        qb = lax.dynamic_slice_in_dim(q_b, q0, Q_BLOCK, axis=1)
        Lb = Q_BLOCK + SWA_WINDOW
        kbw = lax.dynamic_slice_in_dim(kb_pad, q0, Lb, axis=1)
        vbw = lax.dynamic_slice_in_dim(vb_pad, q0, Lb, axis=1)
        kpos_b = q0 - SWA_WINDOW + jnp.arange(Lb)
        dist_b = pos_q[:, None] - kpos_b[None, :]
        valid_b = (dist_b >= 0) & (dist_b < SWA_WINDOW) & (kpos_b[None, :] >= 0)
        lg_b = jnp.einsum('bqghd,blgd->bghql', qb, kbw, preferred_element_type=f32) * scale
        lg_b = jnp.where(valid_b, lg_b + token_bias(bias_b, dist_b), NEG)
        m = jnp.maximum(lg_b.max(axis=-1, keepdims=True), sink)
        e = jnp.exp(lg_b - m)
        p_b = e / (e.sum(axis=-1, keepdims=True) + jnp.exp(sink - m))
        o_b = jnp.einsum('bghql,blgd->bqghd', p_b.astype(dt), vbw)

        return jnp.concatenate([o_a.reshape(B, Q_BLOCK, NSA_Q_W),
                                o_b.reshape(B, Q_BLOCK, SWA_Q_W)], axis=-1)

    out = lax.map(query_block, jnp.arange(S // Q_BLOCK))
    out = out.transpose(1, 0, 2, 3).reshape(B, S, MIX_WIDTH)
    return out @ w_out


def peer(xn, wq, subkeys, u, v):
    B, S, D = xn.shape
    T = B * S
    xt = xn.reshape(T, D)
    q = (xt @ wq).reshape(T, PEER_HEADS, 2, PEER_QDIM // 2).astype(jnp.float32)
    s1 = jnp.einsum('thd,nd->thn', q[:, :, 0], subkeys[0].astype(jnp.float32))
    s2 = jnp.einsum('thd,nd->thn', q[:, :, 1], subkeys[1].astype(jnp.float32))
    v1, i1 = lax.top_k(s1, PEER_TOPK)
    v2, i2 = lax.top_k(s2, PEER_TOPK)
    cand = (v1[..., :, None] + v2[..., None, :]).reshape(T, PEER_HEADS, PEER_TOPK * PEER_TOPK)
    sc, ci = lax.top_k(cand, PEER_TOPK)
    e1 = jnp.take_along_axis(i1, ci // PEER_TOPK, axis=-1)
    e2 = jnp.take_along_axis(i2, ci % PEER_TOPK, axis=-1)
    idx = e1 * PEER_NKEYS + e2
    gate = jax.nn.softmax(sc, axis=-1).astype(xn.dtype)
    n_ch = T // PEER_CHUNK

    def chunk(args):
        xc, ic, gc = args
        h = jax.nn.gelu(jnp.einsum('cd,chkd->chk', xc, u[ic]))
        return jnp.einsum('chk,chkd->cd', gc * h, v[ic])

    out = lax.map(chunk, (xt.reshape(n_ch, PEER_CHUNK, D),
                          idx.reshape(n_ch, PEER_CHUNK, PEER_HEADS, PEER_TOPK),
                          gate.reshape(n_ch, PEER_CHUNK, PEER_HEADS, PEER_TOPK)))
    return out.reshape(B, S, D)


def setup_inputs(seed: int = 0) -> dict:
    key = jax.random.key(seed)
    ks = jax.random.split(key, 18)

    def nrm(k, shape, s):
        return jax.random.normal(k, shape, jnp.float32) * s

    return {
        'x': nrm(ks[0], (BATCH, SEQ, D_MODEL), 1.0),
        'attn_norm': 1.0 + nrm(ks[1], (DEPTH, D_MODEL), 0.02),
        'w_in': nrm(ks[2], (DEPTH, D_MODEL, PROJ_WIDTH), D_MODEL ** -0.5),
        'cmp_pos_k': nrm(ks[3], (DEPTH, CMP_LEN, HEAD_DIM), 0.1),
        'cmp_w1_k': nrm(ks[4], (DEPTH, CMP_LEN * HEAD_DIM, CMP_HIDDEN), (CMP_LEN * HEAD_DIM) ** -0.5),
        'cmp_w2_k': nrm(ks[5], (DEPTH, CMP_HIDDEN, HEAD_DIM), CMP_HIDDEN ** -0.5),
        'cmp_pos_v': nrm(ks[6], (DEPTH, CMP_LEN, HEAD_DIM), 0.1),
        'cmp_w1_v': nrm(ks[7], (DEPTH, CMP_LEN * HEAD_DIM, CMP_HIDDEN), (CMP_LEN * HEAD_DIM) ** -0.5),
        'cmp_w2_v': nrm(ks[8], (DEPTH, CMP_HIDDEN, HEAD_DIM), CMP_HIDDEN ** -0.5),
        'sinks': nrm(ks[9], (DEPTH, SWA_HEADS), 0.5),
        'w_out': nrm(ks[10], (DEPTH, MIX_WIDTH, D_MODEL), MIX_WIDTH ** -0.5),
        'ffn_norm': 1.0 + nrm(ks[11], (DEPTH, D_MODEL), 0.02),
        'peer_wq': nrm(ks[12], (DEPTH, D_MODEL, PEER_HEADS * PEER_QDIM), D_MODEL ** -0.5),
        'peer_subkeys': nrm(ks[13], (DEPTH, 2, PEER_NKEYS, PEER_QDIM // 2), (PEER_QDIM // 2) ** -0.5),
        'peer_u': nrm(ks[14], (DEPTH, PEER_EXPERTS, D_MODEL), D_MODEL ** -0.5),
        'peer_v': nrm(ks[15], (DEPTH, PEER_EXPERTS, D_MODEL), (PEER_HEADS * PEER_TOPK) ** -0.5),
        'rel_bias': nrm(ks[16], (REL_BUCKETS, NSA_HEADS + SWA_HEADS), 0.5),
        'final_norm': 1.0 + nrm(ks[17], (D_MODEL,), 0.02),
    }


def reference(x, attn_norm, w_in, cmp_pos_k, cmp_w1_k, cmp_w2_k, cmp_pos_v, cmp_w1_v, cmp_w2_v,
              sinks, w_out, ffn_norm, peer_wq, peer_subkeys, peer_u, peer_v, rel_bias, final_norm):
    h = x
    for l in range(DEPTH):
        h = h + hybrid_mixer(rmsnorm(h, attn_norm[l]), w_in[l],
                             cmp_pos_k[l], cmp_w1_k[l], cmp_w2_k[l],
                             cmp_pos_v[l], cmp_w1_v[l], cmp_w2_v[l],
                             sinks[l], rel_bias, w_out[l])
        h = h + peer(rmsnorm(h, ffn_norm[l]), peer_wq[l], peer_subkeys[l], peer_u[l], peer_v[l])
    return rmsnorm(h, final_norm)
```

```python
import functools
import math

import jax
import jax.numpy as jnp
import numpy as np
from jax import lax
from jax.experimental import pallas as pl
from jax.experimental.pallas import tpu as pltpu

F32 = jnp.float32
BF16 = jnp.bfloat16

HEAD_DIM = 64
HPG = 8
KV_GROUPS = 2
CMP_STRIDE = 16
CMP_LEN = 32
CMP_HIDDEN = 128
SLC_BLOCK = 64
N_SELECT = 16
NSA_WINDOW = 512
SWA_WINDOW = 128
FORCE_BONUS = 1e4
REL_BUCKETS = 32
REL_EXACT = 16
REL_MAX_DIST = 2048
PEER_HEADS = 8
PEER_NKEYS = 128
PEER_TOPK = 16
QT = 128
KT = 128
NEG = -1e30
REMOVED = -3e38
RMS_EPS = 1e-6
LANE = 128
VMEM_LIMIT = 56 * 1024 * 1024


def _cparams(sem):
    return pltpu.CompilerParams(dimension_semantics=sem, vmem_limit_bytes=VMEM_LIMIT)


def _rel_bucket_np(dist):
    d = np.maximum(dist, 0)
    n_log = REL_BUCKETS - REL_EXACT
    ratio = np.maximum(d, 1).astype(np.float32) / np.float32(REL_EXACT)
    large = REL_EXACT + (np.log(ratio) / np.float32(math.log(REL_MAX_DIST / REL_EXACT))
                         * np.float32(n_log)).astype(np.int32)
    large = np.minimum(large, REL_BUCKETS - 1)
    return np.where(d < REL_EXACT, d, large)


def _far_distance():
    d = np.arange(0, 2 * REL_MAX_DIST)
    b = _rel_bucket_np(d)
    not_last = np.nonzero(b != REL_BUCKETS - 1)[0]
    return int(not_last.max()) + 1


D_FAR = _far_distance()
NEAR_TILES = (D_FAR + KT - 1 + QT - 1) // QT + 0
CMP_NEAR_TILES = (D_FAR + CMP_STRIDE * 7 + CMP_LEN - 1 + QT - 1) // QT
CMP_BAND = 8 * CMP_NEAR_TILES


def _bias_tables(rel_bias):
    rb = rel_bias.astype(F32)
    rb = rb - rb[REL_BUCKETS - 1][None, :]
    t = np.arange(NEAR_TILES)[:, None, None]
    r = np.arange(KT)[None, :, None]
    i = np.arange(QT)[None, None, :]
    bucket_tok = _rel_bucket_np(QT * t + i - r)
    tok = jnp.transpose(rb[jnp.asarray(bucket_tok)], (3, 0, 1, 2))
    rr = np.arange(CMP_BAND)[:, None]
    ii = np.arange(QT)[None, :]
    dist_c = QT * (CMP_NEAR_TILES - 1 - rr // 8) + ii - CMP_STRIDE * (rr % 8) - (CMP_LEN - 1)
    bucket_c = _rel_bucket_np(dist_c)
    cmp_tab = jnp.transpose(rb[jnp.asarray(bucket_c)][:, :, :2 * HPG], (2, 0, 1))
    cmp_tab = jnp.where(jnp.asarray(dist_c >= 0)[None], cmp_tab, NEG)
    return tok, cmp_tab


def _rms_matmul_kernel(x_ref, g_ref, w_ref, o_ref, xn_ref, xs_ref):
    @pl.when(pl.program_id(1) == 0)
    def _():
        x = x_ref[...]
        ms = jnp.mean(x * x, axis=-1, keepdims=True)
        y = x * lax.rsqrt(ms + RMS_EPS) * g_ref[...]
        xs_ref[...] = y.astype(xs_ref.dtype)
        xn_ref[...] = y.astype(xn_ref.dtype)
    o_ref[...] = jnp.dot(xs_ref[...], w_ref[...],
                         preferred_element_type=F32).astype(o_ref.dtype)


def rms_matmul(x, g, w, *, tm, tn, out_dtype):
    T, D = x.shape
    N = w.shape[1]
    cdt = w.dtype
    return pl.pallas_call(
        _rms_matmul_kernel,
        grid=(T // tm, N // tn),
        in_specs=[pl.BlockSpec((tm, D), lambda i, j: (i, 0)),
                  pl.BlockSpec((1, D), lambda i, j: (0, 0)),
                  pl.BlockSpec((D, tn), lambda i, j: (0, j))],
        out_specs=[pl.BlockSpec((tm, tn), lambda i, j: (i, j)),
                   pl.BlockSpec((tm, D), lambda i, j: (i, 0))],
        out_shape=[jax.ShapeDtypeStruct((T, N), out_dtype),
                   jax.ShapeDtypeStruct((T, D), cdt)],
        scratch_shapes=[pltpu.VMEM((tm, D), cdt)],
        compiler_params=_cparams(("parallel", "arbitrary")),
        name="rms_matmul",
    )(x, g.reshape(1, D).astype(F32), w)


def _rmsnorm_kernel(x_ref, g_ref, o_ref):
    x = x_ref[...]
    ms = jnp.mean(x * x, axis=-1, keepdims=True)
    o_ref[...] = x * lax.rsqrt(ms + RMS_EPS) * g_ref[...]


def rmsnorm_final(x, g, *, tm):
    T, D = x.shape
    return pl.pallas_call(
        _rmsnorm_kernel,
        grid=(T // tm,),
        in_specs=[pl.BlockSpec((tm, D), lambda i: (i, 0)),
                  pl.BlockSpec((1, D), lambda i: (0, 0))],
        out_specs=pl.BlockSpec((tm, D), lambda i: (i, 0)),
        out_shape=jax.ShapeDtypeStruct((T, D), F32),
        compiler_params=_cparams(("parallel",)),
        name="rmsnorm_final",
    )(x, g.reshape(1, D).astype(F32))


def _compress_kernel(ch_ref, wlo_ref, whi_ref, pos_ref, w1_ref, w2_ref, o_ref):
    ch = ch_ref[...]
    lo = jnp.dot(ch, wlo_ref[...], preferred_element_type=F32)
    hi = jnp.dot(ch, whi_ref[...], preferred_element_type=F32)
    nch = ch.shape[0]
    hi_next = pltpu.roll(hi, nch - 1, 0)
    c = jnp.dot(pos_ref[...], w1_ref[...], preferred_element_type=F32)[0:1]
    hid = lo + hi_next + jnp.concatenate([c, c], axis=1)
    act = jax.nn.gelu(hid)
    o_ref[...] = jnp.dot(act.astype(w2_ref.dtype), w2_ref[...],
                         preferred_element_type=F32).astype(o_ref.dtype)


def compress(kv, pos, w1, w2, cdt):
    B, S, _ = kv.shape
    nch = S // CMP_STRIDE
    G, Dh, HID = KV_GROUPS, HEAD_DIM, CMP_HIDDEN
    ch = kv.reshape(B, nch, CMP_STRIDE * G * Dh)
    w1r = w1.reshape(2, CMP_STRIDE, Dh, HID).astype(cdt)
    halves = []
    for half in range(2):
        cols = []
        for g in range(G):
            z = jnp.zeros((CMP_STRIDE, G, Dh, HID), cdt).at[:, g].set(w1r[half])
            cols.append(z.reshape(CMP_STRIDE * G * Dh, HID))
        halves.append(jnp.concatenate(cols, axis=1))
    w2e = jnp.zeros((G, HID, G, Dh), cdt)
    for g in range(G):
        w2e = w2e.at[g, :, g].set(w2.astype(cdt))
    w2e = w2e.reshape(G * HID, G * Dh)
    pos_row = jnp.zeros((8, CMP_LEN * Dh), cdt).at[0].set(pos.reshape(-1).astype(cdt))
    kdim = CMP_STRIDE * G * Dh
    return pl.pallas_call(
        _compress_kernel,
        grid=(B,),
        in_specs=[pl.BlockSpec((None, nch, kdim), lambda b: (b, 0, 0)),
                  pl.BlockSpec((kdim, G * HID), lambda b: (0, 0)),
                  pl.BlockSpec((kdim, G * HID), lambda b: (0, 0)),
                  pl.BlockSpec((8, CMP_LEN * Dh), lambda b: (0, 0)),
                  pl.BlockSpec((CMP_LEN * Dh, HID), lambda b: (0, 0)),
                  pl.BlockSpec((G * HID, G * Dh), lambda b: (0, 0))],
        out_specs=pl.BlockSpec((None, nch, G * Dh), lambda b: (b, 0, 0)),
        out_shape=jax.ShapeDtypeStruct((B, nch, G * Dh), cdt),
        compiler_params=_cparams(("parallel",)),
        name="compress",
    )(ch, halves[0], halves[1], pos_row, w1.astype(cdt), w2e)


_NT = (((1,), (1,)), ((), ()))


def _topk_rows(x, k):
    rows = x.shape[0]
    ridx = lax.broadcasted_iota(jnp.int32, x.shape, 0)
    picked = jnp.zeros(x.shape, jnp.bool_)
    vals, idxs = [], []
    for _ in range(k):
        m = jnp.max(x, axis=0, keepdims=True)
        idx = jnp.min(jnp.where(x == m, ridx, rows), axis=0, keepdims=True)
        hit = ridx == idx
        vals.append(m)
        idxs.append(idx)
        picked = picked | hit
        x = jnp.where(hit, REMOVED, x)
    return jnp.concatenate(vals, axis=0), jnp.concatenate(idxs, axis=0), picked


def _online_update(s, vt, m_scr, acc_scr):
    m_prev = m_scr[...]
    m_new = jnp.maximum(m_prev, jnp.max(s, axis=0, keepdims=True))
    alpha = jnp.exp(m_prev - m_new)
    p = jnp.exp(s - m_new)
    pv = jnp.dot(vt, p.astype(vt.dtype), preferred_element_type=F32)
    acc_scr[...] = alpha * acc_scr[...] + pv
    m_scr[...] = m_new


def _heads_bias(tab_ref, t):
    return jnp.concatenate([tab_ref[h, t] for h in range(HPG)], axis=1)


def _write_heads(o_ref, acc_scr, denom_extra=None):
    num = acc_scr[0:HEAD_DIM, :]
    den = acc_scr[HEAD_DIM:2 * HEAD_DIM, :]
    if denom_extra is not None:
        den = den + denom_extra
    o = num / den
    for h in range(HPG):
        o_ref[h] = o[:, h * QT:(h + 1) * QT].astype(o_ref.dtype)


def _cmp_select_kernel(q_ref, kc_ref, vct_ref, tab_ref, o_ref, sel_ref, s_scr, imp_scr,
                       *, nc, ns, scale):
    n = pl.program_id(2)
    pad = CMP_BAND - 8
    cdt = kc_ref.dtype
    row = lax.broadcasted_iota(jnp.int32, (nc, QT), 0)
    col = lax.broadcasted_iota(jnp.int32, (nc, QT), 1)
    valid = CMP_STRIDE * row + (CMP_LEN - 1) <= QT * n + col
    s_scr[pl.ds(0, pad), :] = jnp.zeros((pad, QT), F32)
    band = pl.multiple_of(8 * n, 8)
    imp = jnp.zeros((nc, QT), F32)
    for h in range(HPG):
        qh = q_ref[h] * scale
        s_scr[pl.ds(pad, nc), :] = lax.dot_general(kc_ref[...], qh, _NT,
                                                   preferred_element_type=F32)
        s_scr[pl.ds(band, CMP_BAND), :] = s_scr[pl.ds(band, CMP_BAND), :] + tab_ref[h]
        s = jnp.where(valid, s_scr[pl.ds(pad, nc), :], NEG)
        m = jnp.max(s, axis=0, keepdims=True)
        p = jnp.where(valid, jnp.exp(s - m), 0.0)
        l = jnp.sum(p, axis=0, keepdims=True)
        p = p * (1.0 / jnp.where(l > 0.0, l, 1.0))
        imp = imp + p
        o_ref[h] = jnp.dot(vct_ref[...], p.astype(cdt),
                           preferred_element_type=F32).astype(o_ref.dtype)

    imp_scr[pl.ds(0, 8), :] = jnp.zeros((8, QT), F32)
    imp_scr[pl.ds(8 + nc, 8), :] = jnp.zeros((8, QT), F32)
    imp_scr[pl.ds(8, nc), :] = imp
    imp_s = jnp.zeros((ns, QT), F32)
    for r, w in zip((-1, 0, 1, 2, 3), (1.0, 2.0, 2.0, 2.0, 1.0)):
        imp_s = imp_s + w * imp_scr[pl.ds(8 + r, ns, stride=4), :]
    j = lax.broadcasted_iota(jnp.int32, (ns, QT), 0)
    qi = lax.broadcasted_iota(jnp.int32, (ns, QT), 1)
    cb = (QT // SLC_BLOCK) * n + qi // SLC_BLOCK
    forced = (j == 0) | (j == cb) | (j == cb - 1)
    score = jnp.where(j > cb, NEG, imp_s + jnp.where(forced, FORCE_BONUS, 0.0))
    _, _, picked = _topk_rows(score, min(N_SELECT, ns))
    sel_ref[...] = jnp.where(picked, 1.0, 0.0)


def cmp_select(q, kc, vct, tab, *, out_dtype):
    B, G, H, S, Dh = q.shape
    nc, ns, nq = S // CMP_STRIDE, S // SLC_BLOCK, S // QT
    kern = functools.partial(_cmp_select_kernel, nc=nc, ns=ns, scale=Dh ** -0.5)
    return pl.pallas_call(
        kern,
        grid=(B, G, nq),
        in_specs=[pl.BlockSpec((None, None, H, QT, Dh), lambda b, g, n: (b, g, 0, n, 0)),
                  pl.BlockSpec((None, None, nc, Dh), lambda b, g, n: (b, g, 0, 0)),
                  pl.BlockSpec((None, None, Dh, nc), lambda b, g, n: (b, g, 0, 0)),
                  pl.BlockSpec((None, H, CMP_BAND, QT), lambda b, g, n: (g, 0, 0, 0))],
        out_specs=[pl.BlockSpec((None, None, None, H, Dh, QT), lambda b, g, n: (b, g, n, 0, 0, 0)),
                   pl.BlockSpec((None, None, None, ns, QT), lambda b, g, n: (b, g, n, 0, 0))],
        out_shape=[jax.ShapeDtypeStruct((B, G, nq, H, Dh, QT), out_dtype),
                   jax.ShapeDtypeStruct((B, G, nq, ns, QT), F32)],
        scratch_shapes=[pltpu.VMEM((CMP_BAND - 8 + nc, QT), F32),
                        pltpu.VMEM((nc + 16, QT), F32)],
        compiler_params=_cparams(("parallel", "parallel", "arbitrary")),
        name="cmp_select",
    )(q, kc, vct, tab)


def _slc_kernel(q_ref, k_ref, vt_ref, sel_ref, tab_ref, o_ref, m_scr, acc_scr, *, scale):
    n = pl.program_id(2)
    q_all = (q_ref[...] * scale).reshape(HPG * QT, HEAD_DIM)
    m_scr[...] = jnp.full(m_scr.shape, NEG, F32)
    acc_scr[...] = jnp.zeros(acc_scr.shape, F32)
    half = SLC_BLOCK
    krow = lax.broadcasted_iota(jnp.int32, (KT, QT), 0)
    qcol = lax.broadcasted_iota(jnp.int32, (KT, QT), 1)

    def tile_mask(kt):
        s2 = sel_ref[kt]
        rows = jnp.concatenate([jnp.broadcast_to(s2[0:1], (half, QT)),
                                jnp.broadcast_to(s2[1:2], (half, QT))], axis=0)
        return rows > 0.5

    def logits(kt):
        k = k_ref[pl.ds(pl.multiple_of(kt * KT, KT), KT), :]
        return lax.dot_general(k, q_all, _NT, preferred_element_type=F32)

    def far_body(kt, carry):
        mask = jnp.tile(tile_mask(kt), (1, HPG))
        _online_update(jnp.where(mask, logits(kt), NEG), vt_ref[kt], m_scr, acc_scr)
        return carry

    def near_body(kt, carry):
        t = n - kt
        mask = tile_mask(kt) & (QT * t + qcol - krow >= 0)
        s = logits(kt) + _heads_bias(tab_ref, t)
        _online_update(jnp.where(jnp.tile(mask, (1, HPG)), s, NEG), vt_ref[kt], m_scr, acc_scr)
        return carry

    n_far = jnp.maximum(n - (NEAR_TILES - 1), 0)
    lax.fori_loop(0, n_far, far_body, 0)
    lax.fori_loop(n_far, n + 1, near_body, 0)
    _write_heads(o_ref, acc_scr)


def slc_attention(q, k, vt, sel, tab, *, out_dtype):
    B, G, H, S, Dh = q.shape
    nq, ns = S // QT, S // SLC_BLOCK
    kern = functools.partial(_slc_kernel, scale=Dh ** -0.5)
    return pl.pallas_call(
        kern,
        grid=(B, G, nq),
        in_specs=[pl.BlockSpec((None, None, H, QT, Dh), lambda b, g, n: (b, g, 0, n, 0)),
                  pl.BlockSpec((None, None, S, Dh), lambda b, g, n: (b, g, 0, 0)),
                  pl.BlockSpec((None, None, S // KT, 2 * Dh, KT), lambda b, g, n: (b, g, 0, 0, 0)),
                  pl.BlockSpec((None, None, None, ns // 2, 2, QT),
                               lambda b, g, n: (b, g, n, 0, 0, 0)),
                  pl.BlockSpec((None, H, NEAR_TILES, KT, QT), lambda b, g, n: (g, 0, 0, 0, 0))],
        out_specs=pl.BlockSpec((None, None, None, H, Dh, QT), lambda b, g, n: (b, g, n, 0, 0, 0)),
        out_shape=jax.ShapeDtypeStruct((B, G, nq, H, Dh, QT), out_dtype),
        scratch_shapes=[pltpu.VMEM((1, H * QT), F32),
                        pltpu.VMEM((2 * Dh, H * QT), F32)],
        compiler_params=_cparams(("parallel", "parallel", "arbitrary")),
        name="slc_attention",
    )(q, k, vt, sel, tab)


def _window_kernel(q_ref, k_ref, vt_ref, tab_ref, sink_ref, o_ref, m_scr, acc_scr,
                   *, n_tiles, scale):
    n = pl.program_id(2)
    q_all = (q_ref[...] * scale).reshape(HPG * QT, HEAD_DIM)
    sink = sink_ref[...]
    m_scr[...] = sink
    acc_scr[...] = jnp.zeros(acc_scr.shape, F32)
    for t in range(n_tiles - 1, -1, -1):
        kt = n - t
        ktc = jnp.maximum(kt, 0)
        k = k_ref[pl.ds(pl.multiple_of(ktc * KT, KT), KT), :]
        s = lax.dot_general(k, q_all, _NT, preferred_element_type=F32) + _heads_bias(tab_ref, t)
        s = jnp.where(kt >= 0, s, NEG)
        _online_update(s, vt_ref[ktc], m_scr, acc_scr)
    _write_heads(o_ref, acc_scr, denom_extra=jnp.exp(sink - m_scr[...]))


def window_attention(q, k, vt, tab, sink_row, *, out_dtype):
    B, G, H, S, Dh = q.shape
    nq = S // QT
    n_tiles = tab.shape[2]
    kern = functools.partial(_window_kernel, n_tiles=n_tiles, scale=Dh ** -0.5)
    return pl.pallas_call(
        kern,
        grid=(B, G, nq),
        in_specs=[pl.BlockSpec((None, None, H, QT, Dh), lambda b, g, n: (b, g, 0, n, 0)),
                  pl.BlockSpec((None, None, S, Dh), lambda b, g, n: (b, g, 0, 0)),
                  pl.BlockSpec((None, None, S // KT, 2 * Dh, KT), lambda b, g, n: (b, g, 0, 0, 0)),
                  pl.BlockSpec((None, H, n_tiles, KT, QT), lambda b, g, n: (g, 0, 0, 0, 0)),
                  pl.BlockSpec((None, 1, H * QT), lambda b, g, n: (g, 0, 0))],
        out_specs=pl.BlockSpec((None, None, None, H, Dh, QT), lambda b, g, n: (b, g, n, 0, 0, 0)),
        out_shape=jax.ShapeDtypeStruct((B, G, nq, H, Dh, QT), out_dtype),
        scratch_shapes=[pltpu.VMEM((1, H * QT), F32),
                        pltpu.VMEM((2 * Dh, H * QT), F32)],
        compiler_params=_cparams(("parallel", "parallel", "arbitrary")),
        name="window_attention",
    )(q, k, vt, tab, sink_row)


def _split_hi_lo(x, cdt):
    hi = x.astype(cdt)
    lo = (x - hi.astype(F32)).astype(cdt)
    return hi, lo


def _out_proj_kernel(oc_ref, os_ref, ow_ref, ob_ref, gate_ref, ex_ref, wa_ref, wb_ref, h_ref,
                     o_ref, oa_scr):
    cdt = wa_ref.dtype

    @pl.when(pl.program_id(1) == 0)
    def _():
        g = jax.nn.sigmoid(gate_ref[...].astype(F32))
        g_hi, g_lo = _split_hi_lo(g, cdt)
        o_a = jnp.zeros(oa_scr.shape, F32)
        for c, br_ref in enumerate((oc_ref, os_ref, ow_ref)):
            ge = (jnp.dot(g_hi, ex_ref[c], preferred_element_type=F32)
                  + jnp.dot(g_lo, ex_ref[c], preferred_element_type=F32))
            o_a = o_a + ge * br_ref[...].astype(F32)
        oa_scr[...] = o_a.astype(cdt)

    o_ref[...] = (h_ref[...]
                  + jnp.dot(oa_scr[...], wa_ref[...], preferred_element_type=F32)
                  + jnp.dot(ob_ref[...], wb_ref[...], preferred_element_type=F32))


def out_proj(o_cmp, o_slc, o_win, o_b, gate, expand, w_a, w_b, h, *, tm, tn):
    T, WA = o_cmp.shape
    D = h.shape[1]
    row = lambda i, j: (i, 0)
    return pl.pallas_call(
        _out_proj_kernel,
        grid=(T // tm, D // tn),
        in_specs=[pl.BlockSpec((tm, WA), row), pl.BlockSpec((tm, WA), row),
                  pl.BlockSpec((tm, WA), row), pl.BlockSpec((tm, WA), row),
                  pl.BlockSpec((tm, LANE), row),
                  pl.BlockSpec((3, LANE, WA), lambda i, j: (0, 0, 0)),
                  pl.BlockSpec((WA, tn), lambda i, j: (0, j)),
                  pl.BlockSpec((WA, tn), lambda i, j: (0, j)),
                  pl.BlockSpec((tm, tn), lambda i, j: (i, j))],
        out_specs=pl.BlockSpec((tm, tn), lambda i, j: (i, j)),
        out_shape=jax.ShapeDtypeStruct((T, D), F32),
        scratch_shapes=[pltpu.VMEM((tm, WA), w_a.dtype)],
        compiler_params=_cparams(("parallel", "arbitrary")),
        name="out_proj",
    )(o_cmp, o_slc, o_win, o_b, gate, expand, w_a, w_b, h)


def _peer_route_kernel(q_ref, sk_ref, w_ref, e1_scr, e2_scr, g_scr, slab_scr):
    cdt = q_ref.dtype
    tt = q_ref.shape[0]
    nk, k = PEER_NKEYS, PEER_TOPK
    half = sk_ref.shape[2]
    e1_rows, e2_rows, g_rows = [], [], []
    for h in range(PEER_HEADS):
        q1 = q_ref[:, 2 * h * half:(2 * h + 1) * half]
        q2 = q_ref[:, (2 * h + 1) * half:(2 * h + 2) * half]
        s1 = lax.dot_general(sk_ref[0], q1, _NT, preferred_element_type=F32)
        s2 = lax.dot_general(sk_ref[1], q2, _NT, preferred_element_type=F32)
        v1, i1, _ = _topk_rows(s1, k)
        v2, i2, _ = _topk_rows(s2, k)
        cand = jnp.concatenate([v1[a:a + 1] + v2 for a in range(k)], axis=0)
        sc, ci, _ = _topk_rows(cand, k)
        ai = jnp.right_shift(ci, int(math.log2(k)))
        bi = jnp.bitwise_and(ci, k - 1)
        e1 = jnp.zeros((k, tt), jnp.int32)
        e2 = jnp.zeros((k, tt), jnp.int32)
        for a in range(k):
            e1 = e1 + jnp.where(ai == a, i1[a:a + 1], 0)
            e2 = e2 + jnp.where(bi == a, i2[a:a + 1], 0)
        ex = jnp.exp(sc - sc[0:1])
        gate = ex / jnp.sum(ex, axis=0, keepdims=True)
        e1_rows.append(e1.astype(F32))
        e2_rows.append(e2.astype(F32))
        g_rows.append(gate)
    e1_scr[...] = jnp.concatenate(e1_rows, axis=0).T
    e2_scr[...] = jnp.concatenate(e2_rows, axis=0).T
    g_scr[...] = jnp.concatenate(g_rows, axis=0).T

    key_id = lax.broadcasted_iota(jnp.int32, (nk, PEER_HEADS * k), 0).astype(F32)

    def per_token(t, carry):
        e1r = e1_scr[pl.ds(t, 1), :]
        e2r = e2_scr[pl.ds(t, 1), :]
        gr = g_scr[pl.ds(t, 1), :]
        a = jnp.where(key_id == e1r, gr, 0.0)
        b = jnp.where(key_id == e2r, 1.0, 0.0).astype(cdt)
        a_hi, a_lo = _split_hi_lo(a, cdt)
        w2 = lax.dot_general(jnp.concatenate([a_hi, a_lo], axis=0), b, _NT,
                             preferred_element_type=F32)
        slab_scr[pl.ds(pl.multiple_of(t * nk, nk), nk), :] = w2[:nk] + w2[nk:]
        return carry

    lax.fori_loop(0, tt, per_token, 0)
    for e1 in range(nk):
        w_ref[:, e1 * nk:(e1 + 1) * nk] = slab_scr[pl.ds(e1, tt, stride=nk), :].astype(w_ref.dtype)


def peer_route(q, subkeys, *, tt):
    T, QW = q.shape
    nk = PEER_NKEYS
    half = subkeys.shape[2]
    slots = PEER_HEADS * PEER_TOPK
    return pl.pallas_call(
        _peer_route_kernel,
        grid=(T // tt,),
        in_specs=[pl.BlockSpec((tt, QW), lambda i: (i, 0)),
                  pl.BlockSpec((2, nk, half), lambda i: (0, 0, 0))],
        out_specs=pl.BlockSpec((tt, nk * nk), lambda i: (i, 0)),
        out_shape=jax.ShapeDtypeStruct((T, nk * nk), q.dtype),
        scratch_shapes=[pltpu.VMEM((tt, slots), F32), pltpu.VMEM((tt, slots), F32),
                        pltpu.VMEM((tt, slots), F32), pltpu.VMEM((tt * nk, nk), F32)],
        compiler_params=_cparams(("parallel",)),
        name="peer_route",
    )(q, subkeys)


def _peer_dense_kernel(xn_ref, ut_ref, v_ref, w_ref, h_ref, o_ref):
    @pl.when(pl.program_id(1) == 0)
    def _():
        o_ref[...] = h_ref[...]
    hid = jnp.dot(xn_ref[...], ut_ref[...], preferred_element_type=F32)
    c = (w_ref[...].astype(F32) * jax.nn.gelu(hid)).astype(v_ref.dtype)
    o_ref[...] += jnp.dot(c, v_ref[...], preferred_element_type=F32)


def peer_dense(xn, ut, v, w, h, *, tt, te):
    T, D = xn.shape
    E = v.shape[0]
    return pl.pallas_call(
        _peer_dense_kernel,
        grid=(T // tt, E // te),
        in_specs=[pl.BlockSpec((tt, D), lambda i, e: (i, 0)),
                  pl.BlockSpec((D, te), lambda i, e: (0, e)),
                  pl.BlockSpec((te, D), lambda i, e: (e, 0)),
                  pl.BlockSpec((tt, te), lambda i, e: (i, e)),
                  pl.BlockSpec((tt, D), lambda i, e: (i, 0))],
        out_specs=pl.BlockSpec((tt, D), lambda i, e: (i, 0)),
        out_shape=jax.ShapeDtypeStruct((T, D), F32),
        compiler_params=_cparams(("parallel", "arbitrary")),
        name="peer_dense",
    )(xn, ut, v, w, h)


def _window_table(rel_bias_heads, window):
    nt = window // KT + 1
    t = np.arange(nt)[:, None, None]
    r = np.arange(KT)[None, :, None]
    i = np.arange(QT)[None, None, :]
    dist = QT * t + i - r
    tab = jnp.transpose(rel_bias_heads.astype(F32)[jnp.asarray(_rel_bucket_np(dist))], (3, 0, 1, 2))
    return jnp.where(jnp.asarray((dist >= 0) & (dist < window))[None], tab, NEG)


def _value_tiles(v, cdt):
    B, S, G, Dh = v.shape
    vt = jnp.transpose(v.reshape(B, S // KT, KT, G, Dh), (0, 3, 1, 4, 2))
    return jnp.concatenate([vt, jnp.ones_like(vt)], axis=3).astype(cdt)


def _heads_first(q, B, S):
    return jnp.transpose(q.reshape(B, S, KV_GROUPS, HPG, HEAD_DIM), (0, 2, 3, 1, 4))


def _tokens_first(o_t, B, S):
    return jnp.transpose(o_t, (0, 2, 5, 1, 3, 4)).reshape(B * S, KV_GROUPS * HPG * HEAD_DIM)


def _tile(n, pref):
    return pref if n % pref == 0 else n


def _forward(x, attn_norm, w_in, cmp_pos_k, cmp_w1_k, cmp_w2_k, cmp_pos_v, cmp_w1_v, cmp_w2_v,
             sinks, w_out, ffn_norm, peer_wq, peer_subkeys, peer_u, peer_v, rel_bias, final_norm,
             cdt):
    B, S, D = x.shape
    T = B * S
    depth = w_in.shape[0]
    G, H, Dh = KV_GROUPS, HPG, HEAD_DIM
    QW = G * H * Dh
    KW = G * Dh
    NGATE = G * H * 3
    nsa_heads = G * H

    o_qa, o_kv, o_gate = 0, QW, QW + 6 * KW
    o_qb = o_gate + NGATE
    o_kvb = o_qb + QW
    proj_w = QW + 6 * KW + QW + 2 * KW + LANE
    c_kv, c_qb, c_kvb, c_gate = QW, QW + 6 * KW, 2 * QW + 6 * KW, 2 * QW + 8 * KW

    tok_tab, cmp_tab = _bias_tables(rel_bias)
    slc_tab = tok_tab[:nsa_heads].reshape(G, H, NEAR_TILES, KT, QT)
    cmp_tab = cmp_tab.reshape(G, H, CMP_BAND, QT)
    win_tab = _window_table(rel_bias[:, :nsa_heads], NSA_WINDOW).reshape(
        G, H, NSA_WINDOW // KT + 1, KT, QT)
    swa_tab = _window_table(rel_bias[:, nsa_heads:], SWA_WINDOW).reshape(
        G, H, SWA_WINDOW // KT + 1, KT, QT)
    no_sink = jnp.full((G, 1, H * QT), NEG, F32)

    ex = np.zeros((3, LANE, QW), np.float32)
    for gh in range(G * H):
        for c in range(3):
            ex[c, gh * 3 + c, gh * Dh:(gh + 1) * Dh] = 1.0
    expand = jnp.asarray(ex, cdt)

    tm = _tile(T, 512)
    h = x.reshape(T, D)
    for l in range(depth):
        wi = w_in[l]
        w_re = jnp.concatenate(
            [wi[:, o_qa:o_qa + QW], wi[:, o_kv:o_kv + 6 * KW], wi[:, o_qb:o_qb + QW],
             wi[:, o_kvb:o_kvb + 2 * KW], wi[:, o_gate:o_gate + NGATE],
             jnp.zeros((D, LANE - NGATE), wi.dtype)], axis=1).astype(cdt)
        proj, _ = rms_matmul(h, attn_norm[l], w_re, tm=tm, tn=_tile(proj_w, 640), out_dtype=cdt)

        q_a = _heads_first(proj[:, :QW], B, S)
        q_b = _heads_first(proj[:, c_qb:c_qb + QW], B, S)
        kv = proj[:, c_kv:c_kv + 6 * KW].reshape(B, S, 6, G, Dh)
        kvb = proj[:, c_kvb:c_kvb + 2 * KW].reshape(B, S, 2, G, Dh)
        gate = proj[:, c_gate:c_gate + LANE]

        kc_c = compress(kv[:, :, 0].reshape(B, S, KW), cmp_pos_k[l], cmp_w1_k[l], cmp_w2_k[l], cdt)
        vc_c = compress(kv[:, :, 1].reshape(B, S, KW), cmp_pos_v[l], cmp_w1_v[l], cmp_w2_v[l], cdt)
        nc = S // CMP_STRIDE
        kc_g = jnp.transpose(kc_c.reshape(B, nc, G, Dh), (0, 2, 1, 3))
        vc_t = jnp.transpose(vc_c.reshape(B, nc, G, Dh), (0, 2, 3, 1))
        o_cmp_t, sel = cmp_select(q_a, kc_g, vc_t, cmp_tab, out_dtype=cdt)
        sel = sel.reshape(B, G, S // QT, S // SLC_BLOCK // 2, 2, QT)

        group_major = lambda a: jnp.transpose(a, (0, 2, 1, 3))
        o_slc_t = slc_attention(q_a, group_major(kv[:, :, 2]), _value_tiles(kv[:, :, 3], cdt),
                                sel, slc_tab, out_dtype=cdt)
        o_win_t = window_attention(q_a, group_major(kv[:, :, 4]), _value_tiles(kv[:, :, 5], cdt),
                                   win_tab, no_sink, out_dtype=cdt)
        sink_row = jnp.repeat(sinks[l].astype(F32).reshape(G, H), QT, axis=1).reshape(G, 1, H * QT)
        o_b_t = window_attention(q_b, group_major(kvb[:, :, 0]), _value_tiles(kvb[:, :, 1], cdt),
                                 swa_tab, sink_row, out_dtype=cdt)

        wo = w_out[l].astype(cdt)
        h = out_proj(_tokens_first(o_cmp_t, B, S), _tokens_first(o_slc_t, B, S),
                     _tokens_first(o_win_t, B, S), _tokens_first(o_b_t, B, S),
                     gate, expand, wo[:QW], wo[QW:], h, tm=tm, tn=_tile(D, 1024))

        pq, xn = rms_matmul(h, ffn_norm[l], peer_wq[l].astype(cdt), tm=tm,
                            tn=_tile(peer_wq.shape[2], 1024), out_dtype=cdt)
        w_route = peer_route(pq, peer_subkeys[l].astype(cdt), tt=LANE)
        h = peer_dense(xn, peer_u[l].T.astype(cdt), peer_v[l].astype(cdt), w_route, h,
                       tt=_tile(T, 1024), te=512)
    return rmsnorm_final(h, final_norm, tm=tm).reshape(B, S, D)


def kernel(x, attn_norm, w_in, cmp_pos_k, cmp_w1_k, cmp_w2_k, cmp_pos_v, cmp_w1_v, cmp_w2_v,
           sinks, w_out, ffn_norm, peer_wq, peer_subkeys, peer_u, peer_v, rel_bias, final_norm):
    return _forward(x, attn_norm, w_in, cmp_pos_k, cmp_w1_k, cmp_w2_k, cmp_pos_v, cmp_w1_v,
                    cmp_w2_v, sinks, w_out, ffn_norm, peer_wq, peer_subkeys, peer_u, peer_v,
                    rel_bias, final_norm, BF16)
```

```python
import functools
import math

import jax
import jax.numpy as jnp
import numpy as np
from jax import lax
from jax.experimental import pallas as pl
from jax.experimental.pallas import tpu as pltpu

F32 = jnp.float32
BF16 = jnp.bfloat16

HEAD_DIM = 64
HPG = 8
KV_GROUPS = 2
CMP_STRIDE = 16
CMP_LEN = 32
CMP_HIDDEN = 128
SLC_BLOCK = 64
N_SELECT = 16
NSA_WINDOW = 512
SWA_WINDOW = 128
FORCE_BONUS = 1e4
REL_BUCKETS = 32
REL_EXACT = 16
REL_MAX_DIST = 2048
PEER_HEADS = 8
PEER_NKEYS = 128
PEER_TOPK = 16
QT = 128
KT = 128
NEG = -1e30
REMOVED = -3e38
RMS_EPS = 1e-6
LANE = 128
VMEM_LIMIT = 56 * 1024 * 1024


def _cparams(sem):
    return pltpu.CompilerParams(dimension_semantics=sem, vmem_limit_bytes=VMEM_LIMIT)


def _rel_bucket_np(dist):
    d = np.maximum(dist, 0)
    n_log = REL_BUCKETS - REL_EXACT
    ratio = np.maximum(d, 1).astype(np.float32) / np.float32(REL_EXACT)
    large = REL_EXACT + (np.log(ratio) / np.float32(math.log(REL_MAX_DIST / REL_EXACT))
                         * np.float32(n_log)).astype(np.int32)
    large = np.minimum(large, REL_BUCKETS - 1)
    return np.where(d < REL_EXACT, d, large)


def _far_distance():
    d = np.arange(0, 2 * REL_MAX_DIST)
    b = _rel_bucket_np(d)
    not_last = np.nonzero(b != REL_BUCKETS - 1)[0]
    return int(not_last.max()) + 1


D_FAR = _far_distance()
NEAR_TILES = (D_FAR + KT - 1 + QT - 1) // QT + 0
CMP_NEAR_TILES = (D_FAR + CMP_STRIDE * 7 + CMP_LEN - 1 + QT - 1) // QT
CMP_BAND = 8 * CMP_NEAR_TILES


def _bias_tables(rel_bias):
    rb = rel_bias.astype(F32)
    rb = rb - rb[REL_BUCKETS - 1][None, :]
    t = np.arange(NEAR_TILES)[:, None, None]
    r = np.arange(KT)[None, :, None]
    i = np.arange(QT)[None, None, :]
    bucket_tok = _rel_bucket_np(QT * t + i - r)
    tok = jnp.transpose(rb[jnp.asarray(bucket_tok)], (3, 0, 1, 2))
    rr = np.arange(CMP_BAND)[:, None]
    ii = np.arange(QT)[None, :]
    dist_c = QT * (CMP_NEAR_TILES - 1 - rr // 8) + ii - CMP_STRIDE * (rr % 8) - (CMP_LEN - 1)
    bucket_c = _rel_bucket_np(dist_c)
    cmp_tab = jnp.transpose(rb[jnp.asarray(bucket_c)][:, :, :2 * HPG], (2, 0, 1))
    cmp_tab = jnp.where(jnp.asarray(dist_c >= 0)[None], cmp_tab, NEG)
    return tok, cmp_tab


def _rms_matmul_kernel(x_ref, g_ref, w_ref, o_ref, xn_ref, xs_ref):
    @pl.when(pl.program_id(1) == 0)
    def _():
        x = x_ref[...]
        ms = jnp.mean(x * x, axis=-1, keepdims=True)
        y = x * lax.rsqrt(ms + RMS_EPS) * g_ref[...]
        xs_ref[...] = y.astype(xs_ref.dtype)
        xn_ref[...] = y.astype(xn_ref.dtype)
    o_ref[...] = jnp.dot(xs_ref[...], w_ref[...],
                         preferred_element_type=F32).astype(o_ref.dtype)


def rms_matmul(x, g, w, *, tm, tn, out_dtype):
    T, D = x.shape
    N = w.shape[1]
    cdt = w.dtype
    return pl.pallas_call(
        _rms_matmul_kernel,
        grid=(T // tm, N // tn),
        in_specs=[pl.BlockSpec((tm, D), lambda i, j: (i, 0)),
                  pl.BlockSpec((1, D), lambda i, j: (0, 0)),
                  pl.BlockSpec((D, tn), lambda i, j: (0, j))],
        out_specs=[pl.BlockSpec((tm, tn), lambda i, j: (i, j)),
                   pl.BlockSpec((tm, D), lambda i, j: (i, 0))],
        out_shape=[jax.ShapeDtypeStruct((T, N), out_dtype),
                   jax.ShapeDtypeStruct((T, D), cdt)],
        scratch_shapes=[pltpu.VMEM((tm, D), cdt)],
        compiler_params=_cparams(("parallel", "arbitrary")),
        name="rms_matmul",
    )(x, g.reshape(1, D).astype(F32), w)


def _rmsnorm_kernel(x_ref, g_ref, o_ref):
    x = x_ref[...]
    ms = jnp.mean(x * x, axis=-1, keepdims=True)
    o_ref[...] = x * lax.rsqrt(ms + RMS_EPS) * g_ref[...]


def rmsnorm_final(x, g, *, tm):
    T, D = x.shape
    return pl.pallas_call(
        _rmsnorm_kernel,
        grid=(T // tm,),
        in_specs=[pl.BlockSpec((tm, D), lambda i: (i, 0)),
                  pl.BlockSpec((1, D), lambda i: (0, 0))],
        out_specs=pl.BlockSpec((tm, D), lambda i: (i, 0)),
        out_shape=jax.ShapeDtypeStruct((T, D), F32),
        compiler_params=_cparams(("parallel",)),
        name="rmsnorm_final",
    )(x, g.reshape(1, D).astype(F32))


def _compress_kernel(ch_ref, wlo_ref, whi_ref, pos_ref, w1_ref, w2_ref, o_ref):
    ch = ch_ref[...]
    lo = jnp.dot(ch, wlo_ref[...], preferred_element_type=F32)
    hi = jnp.dot(ch, whi_ref[...], preferred_element_type=F32)
    nch = ch.shape[0]
    hi_next = pltpu.roll(hi, nch - 1, 0)
    c = jnp.dot(pos_ref[...], w1_ref[...], preferred_element_type=F32)[0:1]
    hid = lo + hi_next + jnp.concatenate([c, c], axis=1)
    act = jax.nn.gelu(hid)
    o_ref[...] = jnp.dot(act.astype(w2_ref.dtype), w2_ref[...],
                         preferred_element_type=F32).astype(o_ref.dtype)


def compress(kv, pos, w1, w2, cdt):
    B, S, _ = kv.shape
    nch = S // CMP_STRIDE
    G, Dh, HID = KV_GROUPS, HEAD_DIM, CMP_HIDDEN
    ch = kv.reshape(B, nch, CMP_STRIDE * G * Dh)
    w1r = w1.reshape(2, CMP_STRIDE, Dh, HID).astype(cdt)
    halves = []
    for half in range(2):
        cols = []
        for g in range(G):
            z = jnp.zeros((CMP_STRIDE, G, Dh, HID), cdt).at[:, g].set(w1r[half])
            cols.append(z.reshape(CMP_STRIDE * G * Dh, HID))
        halves.append(jnp.concatenate(cols, axis=1))
    w2e = jnp.zeros((G, HID, G, Dh), cdt)
    for g in range(G):
        w2e = w2e.at[g, :, g].set(w2.astype(cdt))
    w2e = w2e.reshape(G * HID, G * Dh)
    pos_row = jnp.zeros((8, CMP_LEN * Dh), cdt).at[0].set(pos.reshape(-1).astype(cdt))
    kdim = CMP_STRIDE * G * Dh
    return pl.pallas_call(
        _compress_kernel,
        grid=(B,),
        in_specs=[pl.BlockSpec((None, nch, kdim), lambda b: (b, 0, 0)),
                  pl.BlockSpec((kdim, G * HID), lambda b: (0, 0)),
                  pl.BlockSpec((kdim, G * HID), lambda b: (0, 0)),
                  pl.BlockSpec((8, CMP_LEN * Dh), lambda b: (0, 0)),
                  pl.BlockSpec((CMP_LEN * Dh, HID), lambda b: (0, 0)),
                  pl.BlockSpec((G * HID, G * Dh), lambda b: (0, 0))],
        out_specs=pl.BlockSpec((None, nch, G * Dh), lambda b: (b, 0, 0)),
        out_shape=jax.ShapeDtypeStruct((B, nch, G * Dh), cdt),
        compiler_params=_cparams(("parallel",)),
        name="compress",
    )(ch, halves[0], halves[1], pos_row, w1.astype(cdt), w2e)


_NT = (((1,), (1,)), ((), ()))


def _topk_rows(x, k):
    rows = x.shape[0]
    ridx = lax.broadcasted_iota(jnp.int32, x.shape, 0)
    picked = jnp.zeros(x.shape, jnp.bool_)
    vals, idxs = [], []
    for _ in range(k):
        m = jnp.max(x, axis=0, keepdims=True)
        idx = jnp.min(jnp.where(x == m, ridx, rows), axis=0, keepdims=True)
        hit = ridx == idx
        vals.append(m)
        idxs.append(idx)
        picked = picked | hit
        x = jnp.where(hit, REMOVED, x)
    return jnp.concatenate(vals, axis=0), jnp.concatenate(idxs, axis=0), picked


PAIRS = HPG // 2
PW = 2 * QT
ONES_ROWS = 16
VROWS = HEAD_DIM + ONES_ROWS


def _online_update(s, vts, m_scr, acc_scr):
    m_prev = m_scr[...]
    m_new = jnp.maximum(m_prev, jnp.max(s, axis=0, keepdims=True))
    alpha = jnp.exp(m_prev - m_new)
    p = jnp.exp(s - m_new).astype(vts[0].dtype)
    pv = jnp.dot(vts[0], p[0:KT], preferred_element_type=F32)
    for j in range(1, len(vts)):
        pv = pv + jnp.dot(vts[j], p[j * KT:(j + 1) * KT], preferred_element_type=F32)
    acc_scr[...] = alpha * acc_scr[...] + pv
    m_scr[...] = m_new


def _pair_bias(tab_ref, hp, t):
    return jnp.concatenate([tab_ref[2 * hp, t], tab_ref[2 * hp + 1, t]], axis=1)


def _pair_queries(q_ref, hp):
    return q_ref[2 * hp:2 * hp + 2].reshape(PW, HEAD_DIM)


def _write_heads(o_ref, acc_scrs, sink_ref=None, m_scrs=None):
    for hp in range(PAIRS):
        acc = acc_scrs[hp][...]
        den = acc[HEAD_DIM:HEAD_DIM + 1, :]
        if sink_ref is not None:
            den = den + jnp.exp(sink_ref[hp] - m_scrs[hp][...])
        o = acc[0:HEAD_DIM, :] / den
        for j in range(2):
            o_ref[2 * hp + j] = o[:, j * QT:(j + 1) * QT].astype(o_ref.dtype)


def _attn_scratch():
    return ([pltpu.VMEM((1, PW), F32) for _ in range(PAIRS)]
            + [pltpu.VMEM((VROWS, PW), F32) for _ in range(PAIRS)])


def _cmp_select_kernel(q_ref, kc_ref, vct_ref, tab_ref, o_ref, sel_ref, s_scr, imp_scr,
                       *, nc, ns):
    n = pl.program_id(2)
    pad = CMP_BAND - 8
    cdt = kc_ref.dtype
    row = lax.broadcasted_iota(jnp.int32, (nc, QT), 0)
    col = lax.broadcasted_iota(jnp.int32, (nc, QT), 1)
    valid = CMP_STRIDE * row + (CMP_LEN - 1) <= QT * n + col
    s_scr[pl.ds(0, pad), :] = jnp.zeros((pad, QT), F32)
    band = pl.multiple_of(8 * n, 8)
    imp = jnp.zeros((nc, QT), F32)
    for h in range(HPG):
        s_scr[pl.ds(pad, nc), :] = lax.dot_general(kc_ref[...], q_ref[h], _NT,
                                                   preferred_element_type=F32)
        s_scr[pl.ds(band, CMP_BAND), :] = s_scr[pl.ds(band, CMP_BAND), :] + tab_ref[h]
        s = jnp.where(valid, s_scr[pl.ds(pad, nc), :], NEG)
        m = jnp.max(s, axis=0, keepdims=True)
        p = jnp.where(valid, jnp.exp(s - m), 0.0)
        l = jnp.sum(p, axis=0, keepdims=True)
        p = p * (1.0 / jnp.where(l > 0.0, l, 1.0))
        imp = imp + p
        o_ref[h] = jnp.dot(vct_ref[...], p.astype(cdt),
                           preferred_element_type=F32).astype(o_ref.dtype)

    imp_scr[pl.ds(0, 8), :] = jnp.zeros((8, QT), F32)
    imp_scr[pl.ds(8 + nc, 8), :] = jnp.zeros((8, QT), F32)
    imp_scr[pl.ds(8, nc), :] = imp
    imp_s = jnp.zeros((ns, QT), F32)
    for r, w in zip((-1, 0, 1, 2, 3), (1.0, 2.0, 2.0, 2.0, 1.0)):
        imp_s = imp_s + w * imp_scr[pl.ds(8 + r, ns, stride=4), :]
    j = lax.broadcasted_iota(jnp.int32, (ns, QT), 0)
    qi = lax.broadcasted_iota(jnp.int32, (ns, QT), 1)
    cb = (QT // SLC_BLOCK) * n + qi // SLC_BLOCK
    forced = (j == 0) | (j == cb) | (j == cb - 1)
    score = jnp.where(j > cb, NEG, imp_s + jnp.where(forced, FORCE_BONUS, 0.0))
    _, _, picked = _topk_rows(score, min(N_SELECT, ns))
    sel_ref[...] = jnp.where(picked, 1.0, 0.0)


def cmp_select(q, kc, vct, tab, *, out_dtype):
    B, G, H, S, Dh = q.shape
    nc, ns, nq = S // CMP_STRIDE, S // SLC_BLOCK, S // QT
    kern = functools.partial(_cmp_select_kernel, nc=nc, ns=ns)
    return pl.pallas_call(
        kern,
        grid=(B, G, nq),
        in_specs=[pl.BlockSpec((None, None, H, QT, Dh), lambda b, g, n: (b, g, 0, n, 0)),
                  pl.BlockSpec((None, None, nc, Dh), lambda b, g, n: (b, g, 0, 0)),
                  pl.BlockSpec((None, None, Dh, nc), lambda b, g, n: (b, g, 0, 0)),
                  pl.BlockSpec((None, H, CMP_BAND, QT), lambda b, g, n: (g, 0, 0, 0))],
        out_specs=[pl.BlockSpec((None, None, None, H, Dh, QT), lambda b, g, n: (b, g, n, 0, 0, 0)),
                   pl.BlockSpec((None, None, None, ns, QT), lambda b, g, n: (b, g, n, 0, 0))],
        out_shape=[jax.ShapeDtypeStruct((B, G, nq, H, Dh, QT), out_dtype),
                   jax.ShapeDtypeStruct((B, G, nq, ns, QT), F32)],
        scratch_shapes=[pltpu.VMEM((CMP_BAND - 8 + nc, QT), F32),
                        pltpu.VMEM((nc + 16, QT), F32)],
        compiler_params=_cparams(("parallel", "parallel", "arbitrary")),
        name="cmp_select",
    )(q, kc, vct, tab)


SLC_TILE = 512
SLC_SUBS = SLC_TILE // KT


FAR_UNROLL = 2
TAB_ZERO = NEAR_TILES
TAB_FUTURE = NEAR_TILES + 1


def _slc_kernel(q_ref, k_ref, vt_ref, sel_ref, tab_ref, o_ref, *scr):
    m_scrs, acc_scrs = scr[:PAIRS], scr[PAIRS:]
    n = pl.program_id(2)
    for hp in range(PAIRS):
        m_scrs[hp][...] = jnp.full((1, PW), NEG, F32)
        acc_scrs[hp][...] = jnp.zeros((VROWS, PW), F32)

    def tile_mask(kt):
        s8 = jnp.tile(sel_ref[kt], (1, 2))
        rows = [jnp.broadcast_to(s8[b:b + 1], (SLC_BLOCK, PW))
                for b in range(SLC_TILE // SLC_BLOCK)]
        return jnp.concatenate(rows, axis=0) > 0.5

    def run_tiles(kts, near):
        masks = [tile_mask(kt) for kt in kts]
        ks = [k_ref[pl.ds(pl.multiple_of(kt * SLC_TILE, SLC_TILE), SLC_TILE), :] for kt in kts]
        vts = [[vt_ref[SLC_SUBS * kt + j] for j in range(SLC_SUBS)] for kt in kts]
        chains = [(i, hp) for i in range(len(kts)) for hp in range(PAIRS)]
        logits = lambda c: lax.dot_general(ks[c[0]], _pair_queries(q_ref, c[1]), _NT,
                                           preferred_element_type=F32)
        s_next = logits(chains[0])
        for ci, (i, hp) in enumerate(chains):
            s = s_next
            if ci + 1 < len(chains):
                s_next = logits(chains[ci + 1])
            if near:
                offs = [n - (SLC_SUBS * kts[i] + j) for j in range(SLC_SUBS)]
                offs = [jnp.where(t < 0, TAB_FUTURE, jnp.minimum(t, TAB_ZERO)) for t in offs]
                s = s + jnp.concatenate([_pair_bias(tab_ref, hp, t) for t in offs], axis=0)
            _online_update(jnp.where(masks[i], s, NEG), vts[i], m_scrs[hp], acc_scrs[hp])

    def far_body(i, carry):
        run_tiles([FAR_UNROLL * i + u for u in range(FAR_UNROLL)], False)
        return carry

    def far_rest_body(kt, carry):
        run_tiles([kt], False)
        return carry

    def near_body(kt, carry):
        run_tiles([kt], True)
        return carry

    n_far = jnp.maximum((n - (NEAR_TILES - 1)) // SLC_SUBS, 0)
    n_trips = n_far // FAR_UNROLL
    lax.fori_loop(0, n_trips, far_body, 0)
    lax.fori_loop(n_trips * FAR_UNROLL, n_far, far_rest_body, 0)
    lax.fori_loop(n_far, n // SLC_SUBS + 1, near_body, 0)
    _write_heads(o_ref, acc_scrs)


def slc_attention(q, k, vt, sel, tab, *, out_dtype):
    B, G, H, S, Dh = q.shape
    nq = S // QT
    blocks = SLC_TILE // SLC_BLOCK
    return pl.pallas_call(
        _slc_kernel,
        grid=(B, G, nq),
        in_specs=[pl.BlockSpec((None, None, H, QT, Dh), lambda b, g, n: (b, g, 0, n, 0)),
                  pl.BlockSpec((None, None, S, Dh), lambda b, g, n: (b, g, 0, 0)),
                  pl.BlockSpec((None, None, S // KT, VROWS, KT), lambda b, g, n: (b, g, 0, 0, 0)),
                  pl.BlockSpec((None, None, None, S // SLC_TILE, blocks, QT),
                               lambda b, g, n: (b, g, n, 0, 0, 0)),
                  pl.BlockSpec((None, H, NEAR_TILES + 2, KT, QT), lambda b, g, n: (g, 0, 0, 0, 0))],
        out_specs=pl.BlockSpec((None, None, None, H, Dh, QT), lambda b, g, n: (b, g, n, 0, 0, 0)),
        out_shape=jax.ShapeDtypeStruct((B, G, nq, H, Dh, QT), out_dtype),
        scratch_shapes=_attn_scratch(),
        compiler_params=_cparams(("parallel", "parallel", "arbitrary")),
        name="slc_attention",
    )(q, k, vt, sel, tab)


def _window_kernel(q_ref, k_ref, vt_ref, tab_ref, sink_ref, o_ref, *scr, n_tiles):
    m_scrs, acc_scrs = scr[:PAIRS], scr[PAIRS:]
    n = pl.program_id(2)
    rows = n_tiles * KT
    k = k_ref[pl.ds(pl.multiple_of(n * KT, KT), rows), :]
    vts = [vt_ref[n + j] for j in range(n_tiles)]
    real = lax.broadcasted_iota(jnp.int32, (rows, PW), 0) >= (rows - QT) - QT * n
    for hp in range(PAIRS):
        m_scrs[hp][...] = sink_ref[hp]
        acc_scrs[hp][...] = jnp.zeros((VROWS, PW), F32)
        s = lax.dot_general(k, _pair_queries(q_ref, hp), _NT, preferred_element_type=F32)
        s = s + jnp.concatenate([tab_ref[2 * hp], tab_ref[2 * hp + 1]], axis=1)
        _online_update(jnp.where(real, s, NEG), vts, m_scrs[hp], acc_scrs[hp])
    _write_heads(o_ref, acc_scrs, sink_ref, m_scrs)


def window_attention(q, k, vt, tab, sink_row, *, out_dtype):
    B, G, H, S, Dh = q.shape
    nq = S // QT
    rows = tab.shape[2]
    kern = functools.partial(_window_kernel, n_tiles=rows // KT)
    return pl.pallas_call(
        kern,
        grid=(B, G, nq),
        in_specs=[pl.BlockSpec((None, None, H, QT, Dh), lambda b, g, n: (b, g, 0, n, 0)),
                  pl.BlockSpec((None, None, k.shape[2], Dh), lambda b, g, n: (b, g, 0, 0)),
                  pl.BlockSpec((None, None, vt.shape[2], VROWS, KT), lambda b, g, n: (b, g, 0, 0, 0)),
                  pl.BlockSpec((None, H, rows, QT), lambda b, g, n: (g, 0, 0, 0)),
                  pl.BlockSpec((None, PAIRS, 1, PW), lambda b, g, n: (g, 0, 0, 0))],
        out_specs=pl.BlockSpec((None, None, None, H, Dh, QT), lambda b, g, n: (b, g, n, 0, 0, 0)),
        out_shape=jax.ShapeDtypeStruct((B, G, nq, H, Dh, QT), out_dtype),
        scratch_shapes=_attn_scratch(),
        compiler_params=_cparams(("parallel", "parallel", "arbitrary")),
        name="window_attention",
    )(q, k, vt, tab, sink_row)


def _split_hi_lo(x, cdt):
    hi = x.astype(cdt)
    lo = (x - hi.astype(F32)).astype(cdt)
    return hi, lo


def _out_proj_kernel(oc_ref, os_ref, ow_ref, ob_ref, gate_ref, ex_ref, wa_ref, wb_ref, h_ref,
                     o_ref, oa_scr):
    cdt = wa_ref.dtype

    @pl.when(pl.program_id(1) == 0)
    def _():
        g = jax.nn.sigmoid(gate_ref[...].astype(F32))
        g_hi, g_lo = _split_hi_lo(g, cdt)
        o_a = jnp.zeros(oa_scr.shape, F32)
        for c, br_ref in enumerate((oc_ref, os_ref, ow_ref)):
            ge = (jnp.dot(g_hi, ex_ref[c], preferred_element_type=F32)
                  + jnp.dot(g_lo, ex_ref[c], preferred_element_type=F32))
            o_a = o_a + ge * br_ref[...].astype(F32)
        oa_scr[...] = o_a.astype(cdt)

    o_ref[...] = (h_ref[...]
                  + jnp.dot(oa_scr[...], wa_ref[...], preferred_element_type=F32)
                  + jnp.dot(ob_ref[...], wb_ref[...], preferred_element_type=F32))


def out_proj(o_cmp, o_slc, o_win, o_b, gate, expand, w_a, w_b, h, *, tm, tn):
    T, WA = o_cmp.shape
    D = h.shape[1]
    row = lambda i, j: (i, 0)
    return pl.pallas_call(
        _out_proj_kernel,
        grid=(T // tm, D // tn),
        in_specs=[pl.BlockSpec((tm, WA), row), pl.BlockSpec((tm, WA), row),
                  pl.BlockSpec((tm, WA), row), pl.BlockSpec((tm, WA), row),
                  pl.BlockSpec((tm, LANE), row),
                  pl.BlockSpec((3, LANE, WA), lambda i, j: (0, 0, 0)),
                  pl.BlockSpec((WA, tn), lambda i, j: (0, j)),
                  pl.BlockSpec((WA, tn), lambda i, j: (0, j)),
                  pl.BlockSpec((tm, tn), lambda i, j: (i, j))],
        out_specs=pl.BlockSpec((tm, tn), lambda i, j: (i, j)),
        out_shape=jax.ShapeDtypeStruct((T, D), F32),
        scratch_shapes=[pltpu.VMEM((tm, WA), w_a.dtype)],
        compiler_params=_cparams(("parallel", "arbitrary")),
        name="out_proj",
    )(o_cmp, o_slc, o_win, o_b, gate, expand, w_a, w_b, h)


_CAND_NB = tuple(PEER_TOPK // (a + 1) for a in range(PEER_TOPK))
_CAND_START = tuple(int(v) for v in np.cumsum((0,) + _CAND_NB[:-1]))
_CAND_ROWS = sum(_CAND_NB)
_CAND_PAD = -(-_CAND_ROWS // 8) * 8
PEER_TOKEN_UNROLL = 8
SLAB_PITCH = LANE + 8


def _peer_route_kernel(q_ref, sk_ref, w_ref, cand_scr, e1t_scr, e2t_scr, gt_scr,
                       e1_scr, e2_scr, g_scr, slab_scr):
    cdt = q_ref.dtype
    tt = q_ref.shape[1]
    nk, k = PEER_NKEYS, PEER_TOPK
    cand_scr[pl.ds(_CAND_ROWS, _CAND_PAD - _CAND_ROWS), :] = jnp.full(
        (_CAND_PAD - _CAND_ROWS, tt), REMOVED, F32)

    def per_head(h, carry):
        s1 = lax.dot_general(sk_ref[0], q_ref[2 * h], _NT, preferred_element_type=F32)
        s2 = lax.dot_general(sk_ref[1], q_ref[2 * h + 1], _NT, preferred_element_type=F32)
        v1, i1, _ = _topk_rows(s1, k)
        v2, i2, _ = _topk_rows(s2, k)
        for a in range(k):
            cand_scr[pl.ds(_CAND_START[a], _CAND_NB[a]), :] = v1[a:a + 1] + v2[0:_CAND_NB[a]]
        sc, ci, _ = _topk_rows(cand_scr[...], k)
        ai = jnp.zeros((k, tt), jnp.int32)
        bi = ci
        for a in range(1, k):
            later = ci >= _CAND_START[a]
            ai = ai + jnp.where(later, 1, 0)
            bi = bi - jnp.where(later, _CAND_NB[a - 1], 0)
        e1 = jnp.zeros((k, tt), jnp.int32)
        e2 = jnp.zeros((k, tt), jnp.int32)
        for a in range(k):
            e1 = e1 + jnp.where(ai == a, i1[a:a + 1], 0)
            e2 = e2 + jnp.where(bi == a, i2[a:a + 1], 0)
        ex = jnp.exp(sc - sc[0:1])
        rows = pl.ds(pl.multiple_of(h * k, k), k)
        e1t_scr[rows, :] = e1.astype(F32)
        e2t_scr[rows, :] = e2.astype(F32)
        gt_scr[rows, :] = ex / jnp.sum(ex, axis=0, keepdims=True)
        return carry

    lax.fori_loop(0, PEER_HEADS, per_head, 0)
    e1_scr[...] = e1t_scr[...].T
    e2_scr[...] = e2t_scr[...].T
    g_scr[...] = gt_scr[...].T

    key_id = lax.broadcasted_iota(jnp.int32, (nk, PEER_HEADS * k), 0).astype(F32)

    def per_token(t, carry):
        e1r = e1_scr[pl.ds(t, 1), :]
        e2r = e2_scr[pl.ds(t, 1), :]
        gr = g_scr[pl.ds(t, 1), :]
        a = jnp.where(key_id == e1r, gr, 0.0)
        b = jnp.where(key_id == e2r, 1.0, 0.0).astype(cdt)
        a_hi, a_lo = _split_hi_lo(a, cdt)
        w2 = lax.dot_general(jnp.concatenate([a_hi, a_lo], axis=0), b, _NT,
                             preferred_element_type=F32)
        slab_scr[pl.ds(t, nk, stride=SLAB_PITCH), :] = w2[:nk] + w2[nk:]
        return carry

    lax.fori_loop(0, tt, per_token, 0, unroll=PEER_TOKEN_UNROLL)
    for e1 in range(nk):
        w_ref[:, e1 * nk:(e1 + 1) * nk] = slab_scr[pl.ds(e1 * SLAB_PITCH, tt), :].astype(w_ref.dtype)


def peer_route(q, subkeys, *, tt):
    nq, T, half = q.shape
    nk = PEER_NKEYS
    slots = PEER_HEADS * PEER_TOPK
    return pl.pallas_call(
        _peer_route_kernel,
        grid=(T // tt,),
        in_specs=[pl.BlockSpec((nq, tt, half), lambda i: (0, i, 0)),
                  pl.BlockSpec((2, nk, half), lambda i: (0, 0, 0))],
        out_specs=pl.BlockSpec((tt, nk * nk), lambda i: (i, 0)),
        out_shape=jax.ShapeDtypeStruct((T, nk * nk), q.dtype),
        scratch_shapes=[pltpu.VMEM((_CAND_PAD, tt), F32),
                        pltpu.VMEM((slots, tt), F32), pltpu.VMEM((slots, tt), F32),
                        pltpu.VMEM((slots, tt), F32),
                        pltpu.VMEM((tt, slots), F32), pltpu.VMEM((tt, slots), F32),
                        pltpu.VMEM((tt, slots), F32), pltpu.VMEM((nk * SLAB_PITCH, nk), F32)],
        compiler_params=_cparams(("parallel",)),
        name="peer_route",
    )(q, subkeys)


def _peer_dense_kernel(xn_ref, ut_ref, v_ref, w_ref, h_ref, o_ref):
    @pl.when(pl.program_id(1) == 0)
    def _():
        o_ref[...] = h_ref[...]
    hid = jnp.dot(xn_ref[...], ut_ref[...], preferred_element_type=F32)
    c = (w_ref[...].astype(F32) * jax.nn.gelu(hid)).astype(v_ref.dtype)
    o_ref[...] += jnp.dot(c, v_ref[...], preferred_element_type=F32)


def peer_dense(xn, ut, v, w, h, *, tt, te):
    T, D = xn.shape
    E = v.shape[0]
    return pl.pallas_call(
        _peer_dense_kernel,
        grid=(T // tt, E // te),
        in_specs=[pl.BlockSpec((tt, D), lambda i, e: (i, 0)),
                  pl.BlockSpec((D, te), lambda i, e: (0, e)),
                  pl.BlockSpec((te, D), lambda i, e: (e, 0)),
                  pl.BlockSpec((tt, te), lambda i, e: (i, e)),
                  pl.BlockSpec((tt, D), lambda i, e: (i, 0))],
        out_specs=pl.BlockSpec((tt, D), lambda i, e: (i, 0)),
        out_shape=jax.ShapeDtypeStruct((T, D), F32),
        compiler_params=_cparams(("parallel", "arbitrary")),
        name="peer_dense",
    )(xn, ut, v, w, h)


def _window_table(rel_bias_heads, window):
    r = np.arange(window + QT)[:, None]
    i = np.arange(QT)[None, :]
    dist = window + i - r
    tab = jnp.transpose(rel_bias_heads.astype(F32)[jnp.asarray(_rel_bucket_np(dist))], (2, 0, 1))
    return jnp.where(jnp.asarray((dist >= 0) & (dist < window))[None], tab, NEG)


def _front_pad(a, rows):
    return jnp.pad(a, ((0, 0), (rows, 0)) + ((0, 0),) * (a.ndim - 2))


def _value_tiles(v, cdt):
    B, S, G, Dh = v.shape
    vt = jnp.transpose(v.reshape(B, S // KT, KT, G, Dh), (0, 3, 1, 4, 2))
    ones = jnp.ones(vt.shape[:3] + (ONES_ROWS, KT), vt.dtype)
    return jnp.concatenate([vt, ones], axis=3).astype(cdt)


def _heads_first(q, B, S):
    return jnp.transpose(q.reshape(B, S, KV_GROUPS, HPG, HEAD_DIM), (0, 2, 3, 1, 4))


def _tokens_first(o_t, B, S):
    return jnp.transpose(o_t, (0, 2, 5, 1, 3, 4)).reshape(B * S, KV_GROUPS * HPG * HEAD_DIM)


def _tile(n, pref):
    return pref if n % pref == 0 else n


def _forward(x, attn_norm, w_in, cmp_pos_k, cmp_w1_k, cmp_w2_k, cmp_pos_v, cmp_w1_v, cmp_w2_v,
             sinks, w_out, ffn_norm, peer_wq, peer_subkeys, peer_u, peer_v, rel_bias, final_norm,
             cdt):
    B, S, D = x.shape
    T = B * S
    depth = w_in.shape[0]
    G, H, Dh = KV_GROUPS, HPG, HEAD_DIM
    QW = G * H * Dh
    KW = G * Dh
    NGATE = G * H * 3
    nsa_heads = G * H

    o_qa, o_kv, o_gate = 0, QW, QW + 6 * KW
    o_qb = o_gate + NGATE
    o_kvb = o_qb + QW
    proj_w = QW + 6 * KW + QW + 2 * KW + LANE
    c_kv, c_qb, c_kvb, c_gate = QW, QW + 6 * KW, 2 * QW + 6 * KW, 2 * QW + 8 * KW

    tok_tab, cmp_tab = _bias_tables(rel_bias)
    causal0 = jnp.asarray(np.arange(QT)[None, :] - np.arange(KT)[:, None] >= 0)
    slc_tab = tok_tab[:nsa_heads]
    slc_tab = jnp.concatenate(
        [jnp.where(causal0[None, None], slc_tab[:, :1], NEG), slc_tab[:, 1:],
         jnp.zeros((nsa_heads, 1, KT, QT), F32), jnp.full((nsa_heads, 1, KT, QT), NEG, F32)],
        axis=1).reshape(G, H, NEAR_TILES + 2, KT, QT)
    cmp_tab = cmp_tab.reshape(G, H, CMP_BAND, QT)
    win_tab = _window_table(rel_bias[:, :nsa_heads], NSA_WINDOW).reshape(
        G, H, NSA_WINDOW + QT, QT)
    swa_tab = _window_table(rel_bias[:, nsa_heads:], SWA_WINDOW).reshape(
        G, H, SWA_WINDOW + QT, QT)
    no_sink = jnp.full((G, PAIRS, 1, PW), NEG, F32)
    q_scale = Dh ** -0.5

    ex = np.zeros((3, LANE, QW), np.float32)
    for gh in range(G * H):
        for c in range(3):
            ex[c, gh * 3 + c, gh * Dh:(gh + 1) * Dh] = 1.0
    expand = jnp.asarray(ex, cdt)

    tm = _tile(T, 512)
    h = x.reshape(T, D)
    for l in range(depth):
        wi = w_in[l]
        w_re = jnp.concatenate(
            [wi[:, o_qa:o_qa + QW] * q_scale, wi[:, o_kv:o_kv + 6 * KW],
             wi[:, o_qb:o_qb + QW] * q_scale,
             wi[:, o_kvb:o_kvb + 2 * KW], wi[:, o_gate:o_gate + NGATE],
             jnp.zeros((D, LANE - NGATE), wi.dtype)], axis=1).astype(cdt)
        proj, _ = rms_matmul(h, attn_norm[l], w_re, tm=tm, tn=_tile(proj_w, 640), out_dtype=cdt)

        q_a = _heads_first(proj[:, :QW], B, S)
        q_b = _heads_first(proj[:, c_qb:c_qb + QW], B, S)
        kv = proj[:, c_kv:c_kv + 6 * KW].reshape(B, S, 6, G, Dh)
        kvb = proj[:, c_kvb:c_kvb + 2 * KW].reshape(B, S, 2, G, Dh)
        gate = proj[:, c_gate:c_gate + LANE]

        kc_c = compress(kv[:, :, 0].reshape(B, S, KW), cmp_pos_k[l], cmp_w1_k[l], cmp_w2_k[l], cdt)
        vc_c = compress(kv[:, :, 1].reshape(B, S, KW), cmp_pos_v[l], cmp_w1_v[l], cmp_w2_v[l], cdt)
        nc = S // CMP_STRIDE
        kc_g = jnp.transpose(kc_c.reshape(B, nc, G, Dh), (0, 2, 1, 3))
        vc_t = jnp.transpose(vc_c.reshape(B, nc, G, Dh), (0, 2, 3, 1))
        o_cmp_t, sel = cmp_select(q_a, kc_g, vc_t, cmp_tab, out_dtype=cdt)
        sel = sel.reshape(B, G, S // QT, S // SLC_TILE, SLC_TILE // SLC_BLOCK, QT)

        group_major = lambda a: jnp.transpose(a, (0, 2, 1, 3))
        o_slc_t = slc_attention(q_a, group_major(kv[:, :, 2]), _value_tiles(kv[:, :, 3], cdt),
                                sel, slc_tab, out_dtype=cdt)
        o_win_t = window_attention(q_a, group_major(_front_pad(kv[:, :, 4], NSA_WINDOW)),
                                   _value_tiles(_front_pad(kv[:, :, 5], NSA_WINDOW), cdt),
                                   win_tab, no_sink, out_dtype=cdt)
        sink_row = jnp.repeat(sinks[l].astype(F32).reshape(G, H), QT, axis=1).reshape(G, PAIRS, 1, PW)
        o_b_t = window_attention(q_b, group_major(_front_pad(kvb[:, :, 0], SWA_WINDOW)),
                                 _value_tiles(_front_pad(kvb[:, :, 1], SWA_WINDOW), cdt),
                                 swa_tab, sink_row, out_dtype=cdt)

        wo = w_out[l].astype(cdt)
        h = out_proj(_tokens_first(o_cmp_t, B, S), _tokens_first(o_slc_t, B, S),
                     _tokens_first(o_win_t, B, S), _tokens_first(o_b_t, B, S),
                     gate, expand, wo[:QW], wo[QW:], h, tm=tm, tn=_tile(D, 1024))

        pq, xn = rms_matmul(h, ffn_norm[l], peer_wq[l].astype(cdt), tm=tm,
                            tn=_tile(peer_wq.shape[2], 1024), out_dtype=cdt)
        pq_halves = jnp.transpose(pq.reshape(T, 2 * PEER_HEADS, -1), (1, 0, 2))
        w_route = peer_route(pq_halves, peer_subkeys[l].astype(cdt), tt=LANE)
        h = peer_dense(xn, peer_u[l].T.astype(cdt), peer_v[l].astype(cdt), w_route, h,
                       tt=_tile(T, 1024), te=512)
    return rmsnorm_final(h, final_norm, tm=tm).reshape(B, S, D)


def kernel(x, attn_norm, w_in, cmp_pos_k, cmp_w1_k, cmp_w2_k, cmp_pos_v, cmp_w1_v, cmp_w2_v,
           sinks, w_out, ffn_norm, peer_wq, peer_subkeys, peer_u, peer_v, rel_bias, final_norm):
    return _forward(x, attn_norm, w_in, cmp_pos_k, cmp_w1_k, cmp_w2_k, cmp_pos_v, cmp_w1_v,
                    cmp_w2_v, sinks, w_out, ffn_norm, peer_wq, peer_subkeys, peer_u, peer_v,
                    rel_bias, final_norm, BF16)
```

```python
import functools
import math

import jax
import jax.numpy as jnp
import numpy as np
from jax import lax
from jax.experimental import pallas as pl
from jax.experimental.pallas import tpu as pltpu

F32 = jnp.float32
BF16 = jnp.bfloat16

HEAD_DIM = 64
HPG = 8
KV_GROUPS = 2
CMP_STRIDE = 16
CMP_LEN = 32
CMP_HIDDEN = 128
SLC_BLOCK = 64
N_SELECT = 16
NSA_WINDOW = 512
SWA_WINDOW = 128
FORCE_BONUS = 1e4
REL_BUCKETS = 32
REL_EXACT = 16
REL_MAX_DIST = 2048
PEER_HEADS = 8
PEER_NKEYS = 128
PEER_TOPK = 16
QT = 128
KT = 128
NEG = -1e30
REMOVED = -3e38
RMS_EPS = 1e-6
LANE = 128
VMEM_LIMIT = 56 * 1024 * 1024


def _cparams(sem):
    return pltpu.CompilerParams(dimension_semantics=sem, vmem_limit_bytes=VMEM_LIMIT)


def _rel_bucket_np(dist):
    d = np.maximum(dist, 0)
    n_log = REL_BUCKETS - REL_EXACT
    ratio = np.maximum(d, 1).astype(np.float32) / np.float32(REL_EXACT)
    large = REL_EXACT + (np.log(ratio) / np.float32(math.log(REL_MAX_DIST / REL_EXACT))
                         * np.float32(n_log)).astype(np.int32)
    large = np.minimum(large, REL_BUCKETS - 1)
    return np.where(d < REL_EXACT, d, large)


def _far_distance():
    d = np.arange(0, 2 * REL_MAX_DIST)
    b = _rel_bucket_np(d)
    not_last = np.nonzero(b != REL_BUCKETS - 1)[0]
    return int(not_last.max()) + 1


D_FAR = _far_distance()
NEAR_TILES = (D_FAR + KT - 1 + QT - 1) // QT + 0
CMP_NEAR_TILES = (D_FAR + CMP_STRIDE * 7 + CMP_LEN - 1 + QT - 1) // QT
CMP_BAND = 8 * CMP_NEAR_TILES


def _toeplitz(x, rows, cols):
    H, L = x.shape
    assert L <= cols
    xp = jnp.pad(x, ((0, 0), (0, cols + 1 - L)))
    return jnp.tile(xp, (1, rows))[:, :rows * cols].reshape(H, rows, cols)


def _bias_tables(rel_bias):
    rb = rel_bias.astype(F32)[:, :KV_GROUPS * HPG]
    rb = rb - rb[REL_BUCKETS - 1][None, :]
    span = QT * NEAR_TILES + KT - 1
    by_dist = rb[jnp.asarray(_rel_bucket_np(np.arange(span) - (KT - 1)))].T
    skew = _toeplitz(by_dist, KT, span)
    tok = jnp.stack([skew[:, :, QT * t + KT - 1:QT * t + KT - 1 + QT]
                     for t in range(NEAR_TILES)], axis=1)
    rr = np.arange(CMP_BAND)[:, None]
    ii = np.arange(QT)[None, :]
    dist_c = QT * (CMP_NEAR_TILES - 1 - rr // 8) + ii - CMP_STRIDE * (rr % 8) - (CMP_LEN - 1)
    bucket_c = _rel_bucket_np(dist_c)
    cmp_tab = jnp.transpose(rb[jnp.asarray(bucket_c)][:, :, :2 * HPG], (2, 0, 1))
    cmp_tab = jnp.where(jnp.asarray(dist_c >= 0)[None], cmp_tab, NEG)
    return tok, cmp_tab


def _rms_matmul_kernel(x_ref, g_ref, w_ref, o_ref, xn_ref, xs_ref):
    @pl.when(pl.program_id(1) == 0)
    def _():
        x = x_ref[...]
        ms = jnp.mean(x * x, axis=-1, keepdims=True)
        y = x * lax.rsqrt(ms + RMS_EPS) * g_ref[...]
        xs_ref[...] = y.astype(xs_ref.dtype)
        xn_ref[...] = y.astype(xn_ref.dtype)
    o_ref[...] = jnp.dot(xs_ref[...], w_ref[...],
                         preferred_element_type=F32).astype(o_ref.dtype)


def rms_matmul(x, g, w, *, tm, tn, out_dtype):
    T, D = x.shape
    N = w.shape[1]
    cdt = w.dtype
    return pl.pallas_call(
        _rms_matmul_kernel,
        grid=(T // tm, N // tn),
        in_specs=[pl.BlockSpec((tm, D), lambda i, j: (i, 0)),
                  pl.BlockSpec((1, D), lambda i, j: (0, 0)),
                  pl.BlockSpec((D, tn), lambda i, j: (0, j))],
        out_specs=[pl.BlockSpec((tm, tn), lambda i, j: (i, j)),
                   pl.BlockSpec((tm, D), lambda i, j: (i, 0))],
        out_shape=[jax.ShapeDtypeStruct((T, N), out_dtype),
                   jax.ShapeDtypeStruct((T, D), cdt)],
        scratch_shapes=[pltpu.VMEM((tm, D), cdt)],
        compiler_params=_cparams(("parallel", "arbitrary")),
        name="rms_matmul",
    )(x, g.reshape(1, D).astype(F32), w)


def _rmsnorm_kernel(x_ref, g_ref, o_ref):
    x = x_ref[...]
    ms = jnp.mean(x * x, axis=-1, keepdims=True)
    o_ref[...] = x * lax.rsqrt(ms + RMS_EPS) * g_ref[...]


def rmsnorm_final(x, g, *, tm):
    T, D = x.shape
    return pl.pallas_call(
        _rmsnorm_kernel,
        grid=(T // tm,),
        in_specs=[pl.BlockSpec((tm, D), lambda i: (i, 0)),
                  pl.BlockSpec((1, D), lambda i: (0, 0))],
        out_specs=pl.BlockSpec((tm, D), lambda i: (i, 0)),
        out_shape=jax.ShapeDtypeStruct((T, D), F32),
        compiler_params=_cparams(("parallel",)),
        name="rmsnorm_final",
    )(x, g.reshape(1, D).astype(F32))


def _compress_kernel(ch_ref, wlo_ref, whi_ref, pos_ref, w1_ref, w2_ref, o_ref):
    ch = ch_ref[...]
    lo = jnp.dot(ch, wlo_ref[...], preferred_element_type=F32)
    hi = jnp.dot(ch, whi_ref[...], preferred_element_type=F32)
    nch = ch.shape[0]
    hi_next = pltpu.roll(hi, nch - 1, 0)
    c = jnp.dot(pos_ref[...], w1_ref[...], preferred_element_type=F32)[0:1]
    hid = lo + hi_next + jnp.concatenate([c, c], axis=1)
    act = jax.nn.gelu(hid)
    o_ref[...] = jnp.dot(act.astype(w2_ref.dtype), w2_ref[...],
                         preferred_element_type=F32).astype(o_ref.dtype)


def compress(kv, pos, w1, w2, cdt):
    B, S, _ = kv.shape
    nch = S // CMP_STRIDE
    G, Dh, HID = KV_GROUPS, HEAD_DIM, CMP_HIDDEN
    ch = kv.reshape(B, nch, CMP_STRIDE * G * Dh)
    w1r = w1.reshape(2, CMP_STRIDE, Dh, HID).astype(cdt)
    halves = []
    for half in range(2):
        cols = []
        for g in range(G):
            z = jnp.zeros((CMP_STRIDE, G, Dh, HID), cdt).at[:, g].set(w1r[half])
            cols.append(z.reshape(CMP_STRIDE * G * Dh, HID))
        halves.append(jnp.concatenate(cols, axis=1))
    w2e = jnp.zeros((G, HID, G, Dh), cdt)
    for g in range(G):
        w2e = w2e.at[g, :, g].set(w2.astype(cdt))
    w2e = w2e.reshape(G * HID, G * Dh)
    pos_row = jnp.zeros((8, CMP_LEN * Dh), cdt).at[0].set(pos.reshape(-1).astype(cdt))
    kdim = CMP_STRIDE * G * Dh
    return pl.pallas_call(
        _compress_kernel,
        grid=(B,),
        in_specs=[pl.BlockSpec((None, nch, kdim), lambda b: (b, 0, 0)),
                  pl.BlockSpec((kdim, G * HID), lambda b: (0, 0)),
                  pl.BlockSpec((kdim, G * HID), lambda b: (0, 0)),
                  pl.BlockSpec((8, CMP_LEN * Dh), lambda b: (0, 0)),
                  pl.BlockSpec((CMP_LEN * Dh, HID), lambda b: (0, 0)),
                  pl.BlockSpec((G * HID, G * Dh), lambda b: (0, 0))],
        out_specs=pl.BlockSpec((None, nch, G * Dh), lambda b: (b, 0, 0)),
        out_shape=jax.ShapeDtypeStruct((B, nch, G * Dh), cdt),
        compiler_params=_cparams(("parallel",)),
        name="compress",
    )(ch, halves[0], halves[1], pos_row, w1.astype(cdt), w2e)


_NT = (((1,), (1,)), ((), ()))


def _topk_rows_multi(xs, k):
    xs = list(xs)
    ridx = [lax.broadcasted_iota(jnp.int32, x.shape, 0) for x in xs]
    picked = [jnp.zeros(x.shape, jnp.bool_) for x in xs]
    vals = [[] for _ in xs]
    idxs = [[] for _ in xs]
    for _ in range(k):
        for j, x in enumerate(xs):
            m = jnp.max(x, axis=0, keepdims=True)
            idx = jnp.min(jnp.where(x == m, ridx[j], x.shape[0]), axis=0, keepdims=True)
            hit = ridx[j] == idx
            vals[j].append(m)
            idxs[j].append(idx)
            picked[j] = picked[j] | hit
            xs[j] = jnp.where(hit, REMOVED, x)
    return [(jnp.concatenate(v, axis=0), jnp.concatenate(i, axis=0), p)
            for v, i, p in zip(vals, idxs, picked)]


def _topk_rows(x, k):
    return _topk_rows_multi([x], k)[0]


PAIRS = HPG // 2
PW = 2 * QT
ONES_ROWS = 16
VROWS = HEAD_DIM + ONES_ROWS


def _softmax_step(s_scr, m_tile, m_scr, pdt):
    m_prev = m_scr[...]
    m_new = jnp.maximum(m_prev, m_tile)
    m_scr[...] = m_new
    return jnp.exp(m_prev - m_new), jnp.exp(s_scr[...] - m_new).astype(pdt)


def _value_step(alpha, p, vts, acc_scr):
    pv = jnp.dot(vts[0], p[0:KT], preferred_element_type=F32)
    for j in range(1, len(vts)):
        pv = pv + jnp.dot(vts[j], p[j * KT:(j + 1) * KT], preferred_element_type=F32)
    acc_scr[...] = alpha * acc_scr[...] + pv


LOCKSTEP = 2


def _run_chains(chains, logits_fn, mask_fn, value_tiles_fn, m_scrs, acc_scrs, s_scrs, pdt):
    groups = [chains[j:j + LOCKSTEP] for j in range(0, len(chains), LOCKSTEP)]

    def stage_logits(grp, bank):
        tile_max = []
        for slot, c in enumerate(grp):
            s = mask_fn(c, logits_fn(c))
            s_scrs[bank * LOCKSTEP + slot][...] = s
            tile_max.append(jnp.max(s, axis=0, keepdims=True))
        return tile_max

    max_next = stage_logits(groups[0], 0)
    for gi, grp in enumerate(groups):
        bank = gi % 2
        max_cur = max_next
        if gi + 1 < len(groups):
            max_next = stage_logits(groups[gi + 1], 1 - bank)
        soft = [_softmax_step(s_scrs[bank * LOCKSTEP + slot], max_cur[slot], m_scrs[c[1]], pdt)
                for slot, c in enumerate(grp)]
        for c, (alpha, p) in zip(grp, soft):
            _value_step(alpha, p, value_tiles_fn(c), acc_scrs[c[1]])


def _pair_bias(tab_ref, hp, t):
    return jnp.concatenate([tab_ref[2 * hp, t], tab_ref[2 * hp + 1, t]], axis=1)


def _head_queries(q_ref, h):
    tile = q_ref[:, (h // 2) * LANE:(h // 2 + 1) * LANE]
    lane = lax.broadcasted_iota(jnp.int32, tile.shape, 1)
    mine = (lane >= HEAD_DIM) if h % 2 else (lane < HEAD_DIM)
    return jnp.where(mine, tile, jnp.zeros_like(tile))


def _pair_queries(q_ref, hp):
    return jnp.concatenate([_head_queries(q_ref, 2 * hp), _head_queries(q_ref, 2 * hp + 1)],
                           axis=0)


def _write_heads(o_ref, acc_scrs, sink_ref=None, m_scrs=None):
    for hp in range(PAIRS):
        acc = acc_scrs[hp][...]
        den = acc[HEAD_DIM:HEAD_DIM + 1, :]
        if sink_ref is not None:
            den = den + jnp.exp(sink_ref[hp] - m_scrs[hp][...])
        o = acc[0:HEAD_DIM, :] / den
        both = jnp.concatenate([o[:, :QT], o[:, QT:]], axis=0)
        o_ref[:, hp * LANE:(hp + 1) * LANE] = both.T.astype(o_ref.dtype)


def _attn_scratch(keys):
    return ([pltpu.VMEM((1, PW), F32) for _ in range(PAIRS)]
            + [pltpu.VMEM((VROWS, PW), F32) for _ in range(PAIRS)]
            + [pltpu.VMEM((keys, PW), F32) for _ in range(2 * LOCKSTEP)])


def _split_scratch(scr):
    return scr[:PAIRS], scr[PAIRS:2 * PAIRS], scr[2 * PAIRS:]


CMP_CHUNK = 128


def _cmp_select_kernel(q_ref, kc_ref, vct_ref, tab_ref, o_ref, sel_ref,
                       s_scr, p_scr, o_scr, imp_scr, *, nc, ns):
    n = pl.program_id(2)
    pad = CMP_BAND - 8
    cdt = kc_ref.dtype
    chunks = (8 * n + 8 + CMP_CHUNK - 1) // CMP_CHUNK
    band = pl.multiple_of(8 * n, 8)
    row = lax.broadcasted_iota(jnp.int32, (CMP_CHUNK, QT), 0)
    col = lax.broadcasted_iota(jnp.int32, (CMP_CHUNK, QT), 1)
    alive = (CMP_LEN - 1) <= QT * n + col[0:1]

    for h in range(HPG):
        s_scr[h, pl.ds(0, pad), :] = jnp.zeros((pad, QT), F32)

    def chunk_rows(j):
        return pl.ds(pl.multiple_of(j * CMP_CHUNK, CMP_CHUNK), CMP_CHUNK)

    def padded_rows(j):
        return pl.ds(pl.multiple_of(pad + j * CMP_CHUNK, 8), CMP_CHUNK)

    def logits(j, c):
        kc = kc_ref[chunk_rows(j), :]
        for h in range(HPG):
            s_scr[h, padded_rows(j), :] = lax.dot_general(kc, _head_queries(q_ref, h), _NT,
                                                          preferred_element_type=F32)
        return c

    lax.fori_loop(0, chunks, logits, 0)
    for h in range(HPG):
        s_scr[h, pl.ds(band, CMP_BAND), :] = s_scr[h, pl.ds(band, CMP_BAND), :] + tab_ref[h]

    def masked(j):
        valid = CMP_STRIDE * (j * CMP_CHUNK + row) + (CMP_LEN - 1) <= QT * n + col
        return [jnp.where(valid, s_scr[h, padded_rows(j), :], NEG) for h in range(HPG)]

    def head_max(j, ms):
        return tuple(jnp.maximum(m, jnp.max(s, axis=0, keepdims=True))
                     for m, s in zip(ms, masked(j)))

    ms = lax.fori_loop(0, chunks, head_max, tuple(jnp.full((1, QT), NEG, F32) for _ in range(HPG)))
    for h in range(HPG):
        o_scr[h] = jnp.zeros((HEAD_DIM, QT), F32)

    def values(j, ls):
        out = []
        for h, s in enumerate(masked(j)):
            p = jnp.exp(s - ms[h])
            p_scr[h, chunk_rows(j), :] = p
            o_scr[h] += jnp.dot(vct_ref[j], p.astype(cdt), preferred_element_type=F32)
            out.append(ls[h] + jnp.sum(p, axis=0, keepdims=True))
        return tuple(out)

    ls = lax.fori_loop(0, chunks, values, tuple(jnp.zeros((1, QT), F32) for _ in range(HPG)))
    invs = [jnp.where(alive, 1.0 / l, 0.0) for l in ls]
    for hp in range(PAIRS):
        both = jnp.concatenate([o_scr[2 * hp] * invs[2 * hp],
                                o_scr[2 * hp + 1] * invs[2 * hp + 1]], axis=0)
        o_ref[:, hp * LANE:(hp + 1) * LANE] = both.T.astype(o_ref.dtype)

    imp_scr[...] = jnp.zeros(imp_scr.shape, F32)

    def importance(j, c):
        tot = jnp.zeros((CMP_CHUNK, QT), F32)
        for h in range(HPG):
            tot = tot + p_scr[h, chunk_rows(j), :] * invs[h]
        imp_scr[pl.ds(pl.multiple_of(8 + j * CMP_CHUNK, 8), CMP_CHUNK), :] = tot
        return c

    lax.fori_loop(0, chunks, importance, 0)
    imp_s = jnp.zeros((ns, QT), F32)
    for r, w in zip((-1, 0, 1, 2, 3), (1.0, 2.0, 2.0, 2.0, 1.0)):
        imp_s = imp_s + w * imp_scr[pl.ds(8 + r, ns, stride=4), :]
    j = lax.broadcasted_iota(jnp.int32, (ns, QT), 0)
    qi = lax.broadcasted_iota(jnp.int32, (ns, QT), 1)
    cb = (QT // SLC_BLOCK) * n + qi // SLC_BLOCK
    forced = (j == 0) | (j == cb) | (j == cb - 1)
    score = jnp.where(j > cb, NEG, imp_s + jnp.where(forced, FORCE_BONUS, 0.0))
    _, _, picked = _topk_rows(score, min(N_SELECT, ns))
    sel_ref[...] = jnp.where(picked, 1.0, 0.0)


GW = HPG * HEAD_DIM


def _q_spec(nq, qcol):
    return pl.BlockSpec((QT, GW), lambda b, g, n: (b * nq + n, qcol + g))


def _o_spec(nq):
    return pl.BlockSpec((QT, GW), lambda b, g, n: (b * nq + n, g))


def cmp_select(proj, qcol, kc, vct, tab, *, out_dtype):
    B, G, nc = kc.shape[:3]
    H, Dh = HPG, HEAD_DIM
    S = nc * CMP_STRIDE
    ns, nq = S // SLC_BLOCK, S // QT
    kern = functools.partial(_cmp_select_kernel, nc=nc, ns=ns)
    return pl.pallas_call(
        kern,
        grid=(B, G, nq),
        in_specs=[_q_spec(nq, qcol),
                  pl.BlockSpec((None, None, nc, 2 * Dh), lambda b, g, n: (b, g, 0, 0)),
                  pl.BlockSpec((None, None, nc // CMP_CHUNK, Dh, CMP_CHUNK),
                               lambda b, g, n: (b, g, 0, 0, 0)),
                  pl.BlockSpec((None, H, CMP_BAND, QT), lambda b, g, n: (g, 0, 0, 0))],
        out_specs=[_o_spec(nq),
                   pl.BlockSpec((None, None, None, ns, QT), lambda b, g, n: (b, g, n, 0, 0))],
        out_shape=[jax.ShapeDtypeStruct((B * S, G * GW), out_dtype),
                   jax.ShapeDtypeStruct((B, G, nq, ns, QT), F32)],
        scratch_shapes=[pltpu.VMEM((H, CMP_BAND - 8 + nc, QT), F32),
                        pltpu.VMEM((H, nc, QT), F32),
                        pltpu.VMEM((H, Dh, QT), F32),
                        pltpu.VMEM((nc + 16, QT), F32)],
        compiler_params=_cparams(("parallel", "parallel", "arbitrary")),
        name="cmp_select",
    )(proj, kc, vct, tab)


SLC_TILE = 512
SLC_SUBS = SLC_TILE // KT


FAR_UNROLL = 4
TAB_ZERO = NEAR_TILES
TAB_FUTURE = NEAR_TILES + 1


def _slc_kernel(q_ref, k_ref, vt_ref, sel_ref, tab_ref, o_ref, *scr):
    m_scrs, acc_scrs, s_scrs = _split_scratch(scr)
    n = pl.program_id(2)
    for hp in range(PAIRS):
        m_scrs[hp][...] = jnp.full((1, PW), NEG, F32)
        acc_scrs[hp][...] = jnp.zeros((VROWS, PW), F32)

    def tile_mask(kt):
        s8 = jnp.tile(sel_ref[kt], (1, 2))
        rows = [jnp.broadcast_to(s8[b:b + 1], (SLC_BLOCK, PW))
                for b in range(SLC_TILE // SLC_BLOCK)]
        return jnp.concatenate(rows, axis=0) > 0.5

    def run_tiles(kts, near):
        masks = [tile_mask(kt) for kt in kts]
        ks = [k_ref[pl.ds(pl.multiple_of(kt * SLC_TILE, SLC_TILE), SLC_TILE), :] for kt in kts]
        vts = [[vt_ref[SLC_SUBS * kt + j] for j in range(SLC_SUBS)] for kt in kts]
        chains = [(i, hp) for i in range(len(kts)) for hp in range(PAIRS)]
        logits = lambda c: lax.dot_general(ks[c[0]], _pair_queries(q_ref, c[1]), _NT,
                                           preferred_element_type=F32)

        def masked(c, s):
            i, hp = c
            if near:
                offs = [n - (SLC_SUBS * kts[i] + j) for j in range(SLC_SUBS)]
                offs = [jnp.where(t < 0, TAB_FUTURE, jnp.minimum(t, TAB_ZERO)) for t in offs]
                s = s + jnp.concatenate([_pair_bias(tab_ref, hp, t) for t in offs], axis=0)
            return jnp.where(masks[i], s, NEG)

        _run_chains(chains, logits, masked, lambda c: vts[c[0]], m_scrs, acc_scrs, s_scrs,
                    k_ref.dtype)

    def far_body(i, carry):
        run_tiles([FAR_UNROLL * i + u for u in range(FAR_UNROLL)], False)
        return carry

    def far_rest_body(kt, carry):
        run_tiles([kt], False)
        return carry

    def near_body(kt, carry):
        run_tiles([kt], True)
        return carry

    n_far = jnp.maximum((n - (NEAR_TILES - 1)) // SLC_SUBS, 0)
    n_trips = n_far // FAR_UNROLL
    lax.fori_loop(0, n_trips, far_body, 0)
    lax.fori_loop(n_trips * FAR_UNROLL, n_far, far_rest_body, 0)
    lax.fori_loop(n_far, n // SLC_SUBS + 1, near_body, 0)
    _write_heads(o_ref, acc_scrs)


def slc_attention(proj, qcol, k, vt, sel, tab, *, out_dtype):
    B, G, S = k.shape[:3]
    H, Dh = HPG, HEAD_DIM
    nq = S // QT
    blocks = SLC_TILE // SLC_BLOCK
    return pl.pallas_call(
        _slc_kernel,
        grid=(B, G, nq),
        in_specs=[_q_spec(nq, qcol),
                  pl.BlockSpec((None, None, S, 2 * Dh), lambda b, g, n: (b, g, 0, 0)),
                  pl.BlockSpec((None, None, S // KT, VROWS, KT), lambda b, g, n: (b, g, 0, 0, 0)),
                  pl.BlockSpec((None, None, None, S // SLC_TILE, blocks, QT),
                               lambda b, g, n: (b, g, n, 0, 0, 0)),
                  pl.BlockSpec((None, H, NEAR_TILES + 2, KT, QT), lambda b, g, n: (g, 0, 0, 0, 0))],
        out_specs=_o_spec(nq),
        out_shape=jax.ShapeDtypeStruct((B * S, G * GW), out_dtype),
        scratch_shapes=_attn_scratch(SLC_TILE),
        compiler_params=_cparams(("parallel", "parallel", "arbitrary")),
        name="slc_attention",
    )(proj, k, vt, sel, tab)


def _window_kernel(q_ref, k_ref, vt_ref, tab_ref, sink_ref, o_ref, *scr, n_tiles):
    m_scrs, acc_scrs, s_scrs = _split_scratch(scr)
    n = pl.program_id(2)
    rows = n_tiles * KT
    k = k_ref[pl.ds(pl.multiple_of(n * KT, KT), rows), :]
    vts = [vt_ref[n + j] for j in range(n_tiles)]
    real = lax.broadcasted_iota(jnp.int32, (rows, PW), 0) >= (rows - QT) - QT * n
    for hp in range(PAIRS):
        m_scrs[hp][...] = sink_ref[hp]
        acc_scrs[hp][...] = jnp.zeros((VROWS, PW), F32)

    def logits(c):
        return lax.dot_general(k, _pair_queries(q_ref, c[1]), _NT, preferred_element_type=F32)

    def masked(c, s):
        hp = c[1]
        s = s + jnp.concatenate([tab_ref[2 * hp], tab_ref[2 * hp + 1]], axis=1)
        return jnp.where(real, s, NEG)

    _run_chains([(0, hp) for hp in range(PAIRS)], logits, masked, lambda c: vts,
                m_scrs, acc_scrs, s_scrs, k_ref.dtype)
    _write_heads(o_ref, acc_scrs, sink_ref, m_scrs)


def window_attention(proj, qcol, k, vt, tab, sink_row, *, out_dtype):
    B, G = k.shape[:2]
    H, Dh = HPG, HEAD_DIM
    rows = tab.shape[2]
    S = k.shape[2] - (rows - QT)
    nq = S // QT
    kern = functools.partial(_window_kernel, n_tiles=rows // KT)
    return pl.pallas_call(
        kern,
        grid=(B, G, nq),
        in_specs=[_q_spec(nq, qcol),
                  pl.BlockSpec((None, None, k.shape[2], 2 * Dh), lambda b, g, n: (b, g, 0, 0)),
                  pl.BlockSpec((None, None, vt.shape[2], VROWS, KT), lambda b, g, n: (b, g, 0, 0, 0)),
                  pl.BlockSpec((None, H, rows, QT), lambda b, g, n: (g, 0, 0, 0)),
                  pl.BlockSpec((None, PAIRS, 1, PW), lambda b, g, n: (g, 0, 0, 0))],
        out_specs=_o_spec(nq),
        out_shape=jax.ShapeDtypeStruct((B * S, G * GW), out_dtype),
        scratch_shapes=_attn_scratch(rows),
        compiler_params=_cparams(("parallel", "parallel", "arbitrary")),
        name="window_attention",
    )(proj, k, vt, tab, sink_row)


def _split_hi_lo(x, cdt):
    hi = x.astype(cdt)
    lo = (x - hi.astype(F32)).astype(cdt)
    return hi, lo


def _out_proj_kernel(oc_ref, os_ref, ow_ref, ob_ref, gate_ref, ex_ref, wa_ref, wb_ref, h_ref,
                     o_ref, oa_scr):
    cdt = wa_ref.dtype

    @pl.when(pl.program_id(1) == 0)
    def _():
        g = jax.nn.sigmoid(gate_ref[...].astype(F32))
        g_hi, g_lo = _split_hi_lo(g, cdt)
        o_a = jnp.zeros(oa_scr.shape, F32)
        for c, br_ref in enumerate((oc_ref, os_ref, ow_ref)):
            ge = (jnp.dot(g_hi, ex_ref[c], preferred_element_type=F32)
                  + jnp.dot(g_lo, ex_ref[c], preferred_element_type=F32))
            o_a = o_a + ge * br_ref[...].astype(F32)
        oa_scr[...] = o_a.astype(cdt)

    o_ref[...] = (h_ref[...]
                  + jnp.dot(oa_scr[...], wa_ref[...], preferred_element_type=F32)
                  + jnp.dot(ob_ref[...], wb_ref[...], preferred_element_type=F32))


def out_proj(o_cmp, o_slc, o_win, o_b, gate, expand, w_a, w_b, h, *, tm, tn):
    T, WA = o_cmp.shape
    D = h.shape[1]
    row = lambda i, j: (i, 0)
    return pl.pallas_call(
        _out_proj_kernel,
        grid=(T // tm, D // tn),
        in_specs=[pl.BlockSpec((tm, WA), row), pl.BlockSpec((tm, WA), row),
                  pl.BlockSpec((tm, WA), row), pl.BlockSpec((tm, WA), row),
                  pl.BlockSpec((tm, LANE), row),
                  pl.BlockSpec((3, LANE, WA), lambda i, j: (0, 0, 0)),
                  pl.BlockSpec((WA, tn), lambda i, j: (0, j)),
                  pl.BlockSpec((WA, tn), lambda i, j: (0, j)),
                  pl.BlockSpec((tm, tn), lambda i, j: (i, j))],
        out_specs=pl.BlockSpec((tm, tn), lambda i, j: (i, j)),
        out_shape=jax.ShapeDtypeStruct((T, D), F32),
        scratch_shapes=[pltpu.VMEM((tm, WA), w_a.dtype)],
        compiler_params=_cparams(("parallel", "arbitrary")),
        name="out_proj",
    )(o_cmp, o_slc, o_win, o_b, gate, expand, w_a, w_b, h)


_CAND_NB = tuple(PEER_TOPK // (a + 1) for a in range(PEER_TOPK))
_CAND_START = tuple(int(v) for v in np.cumsum((0,) + _CAND_NB[:-1]))
_CAND_ROWS = sum(_CAND_NB)
_CAND_PAD = -(-_CAND_ROWS // 8) * 8
PEER_TOKEN_UNROLL = 16
PEER_HEADS_PER_TRIP = 2
SLAB_PITCH = LANE + 8


def _peer_route_kernel(q_ref, sk_ref, w_ref, cand_scr, e1t_scr, e2t_scr, gt_scr,
                       e1_scr, e2_scr, g_scr, slab_scr):
    cdt = q_ref.dtype
    tt = q_ref.shape[1]
    nk, k = PEER_NKEYS, PEER_TOPK
    for c in range(PEER_HEADS_PER_TRIP):
        cand_scr[c, pl.ds(_CAND_ROWS, _CAND_PAD - _CAND_ROWS), :] = jnp.full(
            (_CAND_PAD - _CAND_ROWS, tt), REMOVED, F32)

    def per_heads(trip, carry):
        heads = [PEER_HEADS_PER_TRIP * trip + c for c in range(PEER_HEADS_PER_TRIP)]
        scores = []
        for h in heads:
            scores.append(lax.dot_general(sk_ref[0], q_ref[2 * h], _NT,
                                          preferred_element_type=F32))
            scores.append(lax.dot_general(sk_ref[1], q_ref[2 * h + 1], _NT,
                                          preferred_element_type=F32))
        halves = _topk_rows_multi(scores, k)
        for c in range(PEER_HEADS_PER_TRIP):
            v1, v2 = halves[2 * c][0], halves[2 * c + 1][0]
            for a in range(k):
                cand_scr[c, pl.ds(_CAND_START[a], _CAND_NB[a]), :] = (
                    v1[a:a + 1] + v2[0:_CAND_NB[a]])
        best = _topk_rows_multi([cand_scr[c] for c in range(PEER_HEADS_PER_TRIP)], k)
        for c, h in enumerate(heads):
            i1, i2 = halves[2 * c][1], halves[2 * c + 1][1]
            sc, ci, _ = best[c]
            ai = jnp.zeros((k, tt), jnp.int32)
            bi = ci
            for a in range(1, k):
                later = ci >= _CAND_START[a]
                ai = ai + jnp.where(later, 1, 0)
                bi = bi - jnp.where(later, _CAND_NB[a - 1], 0)
            e1 = jnp.zeros((k, tt), jnp.int32)
            e2 = jnp.zeros((k, tt), jnp.int32)
            for a in range(k):
                e1 = e1 + jnp.where(ai == a, i1[a:a + 1], 0)
                e2 = e2 + jnp.where(bi == a, i2[a:a + 1], 0)
            ex = jnp.exp(sc - sc[0:1])
            rows = pl.ds(pl.multiple_of(h * k, k), k)
            e1t_scr[rows, :] = e1.astype(F32)
            e2t_scr[rows, :] = e2.astype(F32)
            gt_scr[rows, :] = ex / jnp.sum(ex, axis=0, keepdims=True)
        return carry

    lax.fori_loop(0, PEER_HEADS // PEER_HEADS_PER_TRIP, per_heads, 0)
    e1_scr[...] = e1t_scr[...].T
    e2_scr[...] = e2t_scr[...].T
    g_scr[...] = gt_scr[...].T

    key_id = lax.broadcasted_iota(jnp.int32, (nk, PEER_HEADS * k), 0).astype(F32)

    def per_token(t, carry):
        e1r = e1_scr[pl.ds(t, 1), :]
        e2r = e2_scr[pl.ds(t, 1), :]
        gr = g_scr[pl.ds(t, 1), :]
        a = jnp.where(key_id == e1r, gr, 0.0).astype(cdt)
        b = jnp.where(key_id == e2r, 1.0, 0.0).astype(cdt)
        slab_scr[pl.ds(t, nk, stride=SLAB_PITCH), :] = lax.dot_general(
            a, b, _NT, preferred_element_type=F32)
        return carry

    lax.fori_loop(0, tt, per_token, 0, unroll=PEER_TOKEN_UNROLL)
    for e1 in range(nk):
        w_ref[:, e1 * nk:(e1 + 1) * nk] = slab_scr[pl.ds(e1 * SLAB_PITCH, tt), :].astype(w_ref.dtype)


def peer_route(q, subkeys, *, tt):
    nq, T, half = q.shape
    nk = PEER_NKEYS
    slots = PEER_HEADS * PEER_TOPK
    return pl.pallas_call(
        _peer_route_kernel,
        grid=(T // tt,),
        in_specs=[pl.BlockSpec((nq, tt, half), lambda i: (0, i, 0)),
                  pl.BlockSpec((2, nk, half), lambda i: (0, 0, 0))],
        out_specs=pl.BlockSpec((tt, nk * nk), lambda i: (i, 0)),
        out_shape=jax.ShapeDtypeStruct((T, nk * nk), q.dtype),
        scratch_shapes=[pltpu.VMEM((PEER_HEADS_PER_TRIP, _CAND_PAD, tt), F32),
                        pltpu.VMEM((slots, tt), F32), pltpu.VMEM((slots, tt), F32),
                        pltpu.VMEM((slots, tt), F32),
                        pltpu.VMEM((tt, slots), F32), pltpu.VMEM((tt, slots), F32),
                        pltpu.VMEM((tt, slots), F32), pltpu.VMEM((nk * SLAB_PITCH, nk), F32)],
        compiler_params=_cparams(("parallel",)),
        name="peer_route",
    )(q, subkeys)


def _peer_dense_kernel(xn_ref, ut_ref, v_ref, w_ref, h_ref, o_ref):
    @pl.when(pl.program_id(1) == 0)
    def _():
        o_ref[...] = h_ref[...]
    hid = jnp.dot(xn_ref[...], ut_ref[...], preferred_element_type=F32)
    c = (w_ref[...].astype(F32) * jax.nn.gelu(hid)).astype(v_ref.dtype)
    o_ref[...] += jnp.dot(c, v_ref[...], preferred_element_type=F32)


def peer_dense(xn, ut, v, w, h, *, tt, te):
    T, D = xn.shape
    E = v.shape[0]
    return pl.pallas_call(
        _peer_dense_kernel,
        grid=(T // tt, E // te),
        in_specs=[pl.BlockSpec((tt, D), lambda i, e: (i, 0)),
                  pl.BlockSpec((D, te), lambda i, e: (0, e)),
                  pl.BlockSpec((te, D), lambda i, e: (e, 0)),
                  pl.BlockSpec((tt, te), lambda i, e: (i, e)),
                  pl.BlockSpec((tt, D), lambda i, e: (i, 0))],
        out_specs=pl.BlockSpec((tt, D), lambda i, e: (i, 0)),
        out_shape=jax.ShapeDtypeStruct((T, D), F32),
        compiler_params=_cparams(("parallel", "arbitrary")),
        name="peer_dense",
    )(xn, ut, v, w, h)


def _window_table(rel_bias_heads, window):
    keys = window + QT
    r = np.arange(keys)[:, None]
    i = np.arange(QT)[None, :]
    dist = window + i - r
    span = keys + QT - 1
    y = rel_bias_heads.astype(F32)[jnp.asarray(_rel_bucket_np(window + QT - 1 - np.arange(span)))].T
    skew = _toeplitz(y, QT, span)
    tab = jnp.transpose(skew[:, :, QT - 1:QT - 1 + keys], (0, 2, 1))
    return jnp.where(jnp.asarray((dist >= 0) & (dist < window))[None], tab, NEG)


def _front_pad(a, rows):
    return jnp.pad(a, ((0, 0), (rows, 0)) + ((0, 0),) * (a.ndim - 2))


def _value_tiles(v, cdt):
    B, S, G, Dh = v.shape
    vt = jnp.transpose(v.reshape(B, S // KT, KT, G, Dh), (0, 3, 1, 4, 2))
    ones = jnp.ones(vt.shape[:3] + (ONES_ROWS, KT), vt.dtype)
    return jnp.concatenate([vt, ones], axis=3).astype(cdt)


def _tile(n, pref):
    return pref if n % pref == 0 else n


def _forward(x, attn_norm, w_in, cmp_pos_k, cmp_w1_k, cmp_w2_k, cmp_pos_v, cmp_w1_v, cmp_w2_v,
             sinks, w_out, ffn_norm, peer_wq, peer_subkeys, peer_u, peer_v, rel_bias, final_norm,
             cdt):
    B, S, D = x.shape
    T = B * S
    depth = w_in.shape[0]
    G, H, Dh = KV_GROUPS, HPG, HEAD_DIM
    QW = G * H * Dh
    KW = G * Dh
    NGATE = G * H * 3
    nsa_heads = G * H

    o_qa, o_kv, o_gate = 0, QW, QW + 6 * KW
    o_qb = o_gate + NGATE
    o_kvb = o_qb + QW
    proj_w = 2 * QW + 6 * KW + 2 * KW + LANE
    c_qb, c_kv, c_kvb, c_gate = QW, 2 * QW, 2 * QW + 6 * KW, 2 * QW + 8 * KW

    tok_tab, cmp_tab = _bias_tables(rel_bias)
    causal0 = jnp.asarray(np.arange(QT)[None, :] - np.arange(KT)[:, None] >= 0)
    slc_tab = tok_tab[:nsa_heads]
    slc_tab = jnp.concatenate(
        [jnp.where(causal0[None, None], slc_tab[:, :1], NEG), slc_tab[:, 1:],
         jnp.zeros((nsa_heads, 1, KT, QT), F32), jnp.full((nsa_heads, 1, KT, QT), NEG, F32)],
        axis=1).reshape(G, H, NEAR_TILES + 2, KT, QT)
    cmp_tab = cmp_tab.reshape(G, H, CMP_BAND, QT)
    win_tab = _window_table(rel_bias[:, :nsa_heads], NSA_WINDOW).reshape(
        G, H, NSA_WINDOW + QT, QT)
    swa_tab = _window_table(rel_bias[:, nsa_heads:], SWA_WINDOW).reshape(
        G, H, SWA_WINDOW + QT, QT)
    no_sink = jnp.full((G, PAIRS, 1, PW), NEG, F32)
    q_scale = Dh ** -0.5

    ex = np.zeros((3, LANE, QW), np.float32)
    for gh in range(G * H):
        for c in range(3):
            ex[c, gh * 3 + c, gh * Dh:(gh + 1) * Dh] = 1.0
    expand = jnp.asarray(ex, cdt)

    tm = _tile(T, 512)
    h = x.reshape(T, D)
    for l in range(depth):
        wi = w_in[l]
        w_re = jnp.concatenate(
            [wi[:, o_qa:o_qa + QW] * q_scale, wi[:, o_qb:o_qb + QW] * q_scale,
             wi[:, o_kv:o_kv + 6 * KW], wi[:, o_kvb:o_kvb + 2 * KW],
             wi[:, o_gate:o_gate + NGATE],
             jnp.zeros((D, LANE - NGATE), wi.dtype)], axis=1).astype(cdt)
        proj, _ = rms_matmul(h, attn_norm[l], w_re, tm=tm, tn=_tile(proj_w, 640), out_dtype=cdt)

        qcol_a, qcol_b = 0, c_qb // GW
        kv = proj[:, c_kv:c_kv + 6 * KW].reshape(B, S, 6, G, Dh)
        kvb = proj[:, c_kvb:c_kvb + 2 * KW].reshape(B, S, 2, G, Dh)
        gate = proj[:, c_gate:c_gate + LANE]

        kc_c = compress(kv[:, :, 0].reshape(B, S, KW), cmp_pos_k[l], cmp_w1_k[l], cmp_w2_k[l], cdt)
        vc_c = compress(kv[:, :, 1].reshape(B, S, KW), cmp_pos_v[l], cmp_w1_v[l], cmp_w2_v[l], cdt)
        nc = S // CMP_STRIDE
        group_keys = lambda a: jnp.tile(jnp.transpose(a, (0, 2, 1, 3)), (1, 1, 1, 2))
        vc_t = jnp.transpose(vc_c.reshape(B, nc // CMP_CHUNK, CMP_CHUNK, G, Dh),
                             (0, 3, 1, 4, 2))
        o_cmp, sel = cmp_select(proj, qcol_a, group_keys(kc_c.reshape(B, nc, G, Dh)), vc_t,
                                cmp_tab, out_dtype=cdt)
        sel = sel.reshape(B, G, S // QT, S // SLC_TILE, SLC_TILE // SLC_BLOCK, QT)

        o_slc = slc_attention(proj, qcol_a, group_keys(kv[:, :, 2]),
                              _value_tiles(kv[:, :, 3], cdt), sel, slc_tab, out_dtype=cdt)
        o_win = window_attention(proj, qcol_a, group_keys(_front_pad(kv[:, :, 4], NSA_WINDOW)),
                                 _value_tiles(_front_pad(kv[:, :, 5], NSA_WINDOW), cdt),
                                 win_tab, no_sink, out_dtype=cdt)
        sink_row = jnp.repeat(sinks[l].astype(F32).reshape(G, H), QT, axis=1).reshape(G, PAIRS, 1, PW)
        o_b = window_attention(proj, qcol_b, group_keys(_front_pad(kvb[:, :, 0], SWA_WINDOW)),
                               _value_tiles(_front_pad(kvb[:, :, 1], SWA_WINDOW), cdt),
                               swa_tab, sink_row, out_dtype=cdt)

        wo = w_out[l].astype(cdt)
        h = out_proj(o_cmp, o_slc, o_win, o_b,
                     gate, expand, wo[:QW], wo[QW:], h, tm=tm, tn=_tile(D, 1024))

        pq, xn = rms_matmul(h, ffn_norm[l], peer_wq[l].astype(cdt), tm=tm,
                            tn=_tile(peer_wq.shape[2], 1024), out_dtype=cdt)
        pq_halves = jnp.transpose(pq.reshape(T, 2 * PEER_HEADS, -1), (1, 0, 2))
        w_route = peer_route(pq_halves, peer_subkeys[l].astype(cdt), tt=LANE)
        h = peer_dense(xn, peer_u[l].T.astype(cdt), peer_v[l].astype(cdt), w_route, h,
                       tt=_tile(T, 1024), te=512)
    return rmsnorm_final(h, final_norm, tm=tm).reshape(B, S, D)


def kernel(x, attn_norm, w_in, cmp_pos_k, cmp_w1_k, cmp_w2_k, cmp_pos_v, cmp_w1_v, cmp_w2_v,
           sinks, w_out, ffn_norm, peer_wq, peer_subkeys, peer_u, peer_v, rel_bias, final_norm):
    return _forward(x, attn_norm, w_in, cmp_pos_k, cmp_w1_k, cmp_w2_k, cmp_pos_v, cmp_w1_v,
                    cmp_w2_v, sinks, w_out, ffn_norm, peer_wq, peer_subkeys, peer_u, peer_v,
                    rel_bias, final_norm, BF16)
```

```python
import functools
import math

import jax
import jax.numpy as jnp
import numpy as np
from jax import lax
from jax.experimental import pallas as pl
from jax.experimental.pallas import tpu as pltpu

F32 = jnp.float32
BF16 = jnp.bfloat16

HEAD_DIM = 64
HPG = 8
KV_GROUPS = 2
CMP_STRIDE = 16
CMP_LEN = 32
CMP_HIDDEN = 128
SLC_BLOCK = 64
N_SELECT = 16
NSA_WINDOW = 512
SWA_WINDOW = 128
FORCE_BONUS = 1e4
REL_BUCKETS = 32
REL_EXACT = 16
REL_MAX_DIST = 2048
PEER_HEADS = 8
PEER_NKEYS = 128
PEER_TOPK = 16
QT = 128
KT = 128
NEG = -1e30
REMOVED = -3e38
RMS_EPS = 1e-6
LANE = 128
VMEM_LIMIT = 56 * 1024 * 1024


def _cparams(sem):
    return pltpu.CompilerParams(dimension_semantics=sem, vmem_limit_bytes=VMEM_LIMIT)


def _rel_bucket_np(dist):
    d = np.maximum(dist, 0)
    n_log = REL_BUCKETS - REL_EXACT
    ratio = np.maximum(d, 1).astype(np.float32) / np.float32(REL_EXACT)
    large = REL_EXACT + (np.log(ratio) / np.float32(math.log(REL_MAX_DIST / REL_EXACT))
                         * np.float32(n_log)).astype(np.int32)
    large = np.minimum(large, REL_BUCKETS - 1)
    return np.where(d < REL_EXACT, d, large)


def _far_distance():
    d = np.arange(0, 2 * REL_MAX_DIST)
    b = _rel_bucket_np(d)
    not_last = np.nonzero(b != REL_BUCKETS - 1)[0]
    return int(not_last.max()) + 1


D_FAR = _far_distance()
NEAR_TILES = (D_FAR + KT - 1 + QT - 1) // QT + 0
CMP_NEAR_TILES = (D_FAR + CMP_STRIDE * 7 + CMP_LEN - 1 + QT - 1) // QT
CMP_BAND = 8 * CMP_NEAR_TILES


def _toeplitz(x, rows, cols):
    H, L = x.shape
    assert L <= cols
    xp = jnp.pad(x, ((0, 0), (0, cols + 1 - L)))
    return jnp.tile(xp, (1, rows))[:, :rows * cols].reshape(H, rows, cols)


def _bias_tables(rel_bias):
    rb = rel_bias.astype(F32)[:, :KV_GROUPS * HPG]
    rb = rb - rb[REL_BUCKETS - 1][None, :]
    span = QT * NEAR_TILES + KT - 1
    by_dist = rb[jnp.asarray(_rel_bucket_np(np.arange(span) - (KT - 1)))].T
    skew = _toeplitz(by_dist, KT, span)
    tok = jnp.stack([skew[:, :, QT * t + KT - 1:QT * t + KT - 1 + QT]
                     for t in range(NEAR_TILES)], axis=1)
    rr = np.arange(CMP_BAND)[:, None]
    ii = np.arange(QT)[None, :]
    dist_c = QT * (CMP_NEAR_TILES - 1 - rr // 8) + ii - CMP_STRIDE * (rr % 8) - (CMP_LEN - 1)
    bucket_c = _rel_bucket_np(dist_c)
    cmp_tab = jnp.transpose(rb[jnp.asarray(bucket_c)][:, :, :2 * HPG], (2, 0, 1))
    cmp_tab = jnp.where(jnp.asarray(dist_c >= 0)[None], cmp_tab, NEG)
    return tok, cmp_tab


def _rms_matmul_kernel(x_ref, g_ref, w_ref, o_ref, *xn_ref, tn):
    x = x_ref[...]
    ms = jnp.mean(x * x, axis=-1, keepdims=True)
    y = (x * lax.rsqrt(ms + RMS_EPS) * g_ref[...]).astype(w_ref.dtype)
    if xn_ref:
        xn_ref[0][...] = y
    for c in range(w_ref.shape[1] // tn):
        o_ref[:, c * tn:(c + 1) * tn] = jnp.dot(
            y, w_ref[:, c * tn:(c + 1) * tn], preferred_element_type=F32).astype(o_ref.dtype)


def rms_matmul(x, g, w, *, tm, tn, out_dtype, with_xn):
    T, D = x.shape
    N = w.shape[1]
    cdt = w.dtype
    out_specs = [pl.BlockSpec((tm, N), lambda i: (i, 0))]
    out_shape = [jax.ShapeDtypeStruct((T, N), out_dtype)]
    if with_xn:
        out_specs.append(pl.BlockSpec((tm, D), lambda i: (i, 0)))
        out_shape.append(jax.ShapeDtypeStruct((T, D), cdt))
    return pl.pallas_call(
        functools.partial(_rms_matmul_kernel, tn=tn),
        grid=(T // tm,),
        in_specs=[pl.BlockSpec((tm, D), lambda i: (i, 0)),
                  pl.BlockSpec((1, D), lambda i: (0, 0)),
                  pl.BlockSpec((D, N), lambda i: (0, 0))],
        out_specs=out_specs,
        out_shape=out_shape,
        compiler_params=_cparams(("parallel",)),
        name="rms_matmul",
    )(x, g.reshape(1, D).astype(F32), w)


def _rmsnorm_kernel(x_ref, g_ref, o_ref):
    x = x_ref[...]
    ms = jnp.mean(x * x, axis=-1, keepdims=True)
    o_ref[...] = x * lax.rsqrt(ms + RMS_EPS) * g_ref[...]


def rmsnorm_final(x, g, *, tm):
    T, D = x.shape
    return pl.pallas_call(
        _rmsnorm_kernel,
        grid=(T // tm,),
        in_specs=[pl.BlockSpec((tm, D), lambda i: (i, 0)),
                  pl.BlockSpec((1, D), lambda i: (0, 0))],
        out_specs=pl.BlockSpec((tm, D), lambda i: (i, 0)),
        out_shape=jax.ShapeDtypeStruct((T, D), F32),
        compiler_params=_cparams(("parallel",)),
        name="rmsnorm_final",
    )(x, g.reshape(1, D).astype(F32))


def _compress_kernel(ch_ref, wlo_ref, whi_ref, pos_ref, w1_ref, w2_ref, o_ref):
    ch = ch_ref[...]
    lo = jnp.dot(ch, wlo_ref[...], preferred_element_type=F32)
    hi = jnp.dot(ch, whi_ref[...], preferred_element_type=F32)
    nch = ch.shape[0]
    hi_next = pltpu.roll(hi, nch - 1, 0)
    c = jnp.dot(pos_ref[...], w1_ref[...], preferred_element_type=F32)[0:1]
    hid = lo + hi_next + jnp.concatenate([c, c], axis=1)
    act = jax.nn.gelu(hid)
    o_ref[...] = jnp.dot(act.astype(w2_ref.dtype), w2_ref[...],
                         preferred_element_type=F32).astype(o_ref.dtype)


def compress(kv, pos, w1, w2, cdt):
    B, S, _ = kv.shape
    nch = S // CMP_STRIDE
    G, Dh, HID = KV_GROUPS, HEAD_DIM, CMP_HIDDEN
    ch = kv.reshape(B, nch, CMP_STRIDE * G * Dh)
    w1r = w1.reshape(2, CMP_STRIDE, Dh, HID).astype(cdt)
    halves = []
    for half in range(2):
        cols = []
        for g in range(G):
            z = jnp.zeros((CMP_STRIDE, G, Dh, HID), cdt).at[:, g].set(w1r[half])
            cols.append(z.reshape(CMP_STRIDE * G * Dh, HID))
        halves.append(jnp.concatenate(cols, axis=1))
    w2e = jnp.zeros((G, HID, G, Dh), cdt)
    for g in range(G):
        w2e = w2e.at[g, :, g].set(w2.astype(cdt))
    w2e = w2e.reshape(G * HID, G * Dh)
    pos_row = jnp.zeros((8, CMP_LEN * Dh), cdt).at[0].set(pos.reshape(-1).astype(cdt))
    kdim = CMP_STRIDE * G * Dh
    return pl.pallas_call(
        _compress_kernel,
        grid=(B,),
        in_specs=[pl.BlockSpec((None, nch, kdim), lambda b: (b, 0, 0)),
                  pl.BlockSpec((kdim, G * HID), lambda b: (0, 0)),
                  pl.BlockSpec((kdim, G * HID), lambda b: (0, 0)),
                  pl.BlockSpec((8, CMP_LEN * Dh), lambda b: (0, 0)),
                  pl.BlockSpec((CMP_LEN * Dh, HID), lambda b: (0, 0)),
                  pl.BlockSpec((G * HID, G * Dh), lambda b: (0, 0))],
        out_specs=pl.BlockSpec((None, nch, G * Dh), lambda b: (b, 0, 0)),
        out_shape=jax.ShapeDtypeStruct((B, nch, G * Dh), cdt),
        compiler_params=_cparams(("parallel",)),
        name="compress",
    )(ch, halves[0], halves[1], pos_row, w1.astype(cdt), w2e)


_NT = (((1,), (1,)), ((), ()))


def _topk_rows_multi(xs, k):
    xs = list(xs)
    ridx = [lax.broadcasted_iota(jnp.int32, x.shape, 0) for x in xs]
    picked = [jnp.zeros(x.shape, jnp.bool_) for x in xs]
    vals = [[] for _ in xs]
    idxs = [[] for _ in xs]
    for _ in range(k):
        for j, x in enumerate(xs):
            m = jnp.max(x, axis=0, keepdims=True)
            idx = jnp.min(jnp.where(x == m, ridx[j], x.shape[0]), axis=0, keepdims=True)
            hit = ridx[j] == idx
            vals[j].append(m)
            idxs[j].append(idx)
            picked[j] = picked[j] | hit
            xs[j] = jnp.where(hit, REMOVED, x)
    return [(jnp.concatenate(v, axis=0), jnp.concatenate(i, axis=0), p)
            for v, i, p in zip(vals, idxs, picked)]


def _topk_rows(x, k):
    return _topk_rows_multi([x], k)[0]


PAIRS = HPG // 2
PW = 2 * QT
ONES_ROWS = 16
VROWS = HEAD_DIM + ONES_ROWS


def _softmax_step(s_scr, m_tile, m_scr, pdt):
    m_prev = m_scr[...]
    m_new = jnp.maximum(m_prev, m_tile)
    m_scr[...] = m_new
    return jnp.exp(m_prev - m_new), jnp.exp(s_scr[...] - m_new).astype(pdt)


def _value_step(alpha, p, vts, acc_scr):
    pv = jnp.dot(vts[0], p[0:KT], preferred_element_type=F32)
    for j in range(1, len(vts)):
        pv = pv + jnp.dot(vts[j], p[j * KT:(j + 1) * KT], preferred_element_type=F32)
    acc_scr[...] = alpha * acc_scr[...] + pv


LOCKSTEP = 2


def _run_chains(chains, logits_fn, mask_fn, value_tiles_fn, m_scrs, acc_scrs, s_scrs, pdt):
    groups = [chains[j:j + LOCKSTEP] for j in range(0, len(chains), LOCKSTEP)]

    def stage_logits(grp, bank):
        tile_max = []
        for slot, c in enumerate(grp):
            s = mask_fn(c, logits_fn(c))
            s_scrs[bank * LOCKSTEP + slot][...] = s
            tile_max.append(jnp.max(s, axis=0, keepdims=True))
        return tile_max

    max_next = stage_logits(groups[0], 0)
    for gi, grp in enumerate(groups):
        bank = gi % 2
        max_cur = max_next
        if gi + 1 < len(groups):
            max_next = stage_logits(groups[gi + 1], 1 - bank)
        soft = [_softmax_step(s_scrs[bank * LOCKSTEP + slot], max_cur[slot], m_scrs[c[1]], pdt)
                for slot, c in enumerate(grp)]
        for c, (alpha, p) in zip(grp, soft):
            _value_step(alpha, p, value_tiles_fn(c), acc_scrs[c[1]])


def _pair_bias(tab_ref, hp, t):
    return jnp.concatenate([tab_ref[2 * hp, t], tab_ref[2 * hp + 1, t]], axis=1)


def _head_queries(q_ref, h):
    tile = q_ref[:, (h // 2) * LANE:(h // 2 + 1) * LANE]
    lane = lax.broadcasted_iota(jnp.int32, tile.shape, 1)
    mine = (lane >= HEAD_DIM) if h % 2 else (lane < HEAD_DIM)
    return jnp.where(mine, tile, jnp.zeros_like(tile))


def _pair_queries(q_ref, hp):
    return jnp.concatenate([_head_queries(q_ref, 2 * hp), _head_queries(q_ref, 2 * hp + 1)],
                           axis=0)


def _write_heads(o_ref, acc_scrs, sink_ref=None, m_scrs=None):
    for hp in range(PAIRS):
        acc = acc_scrs[hp][...]
        den = acc[HEAD_DIM:HEAD_DIM + 1, :]
        if sink_ref is not None:
            den = den + jnp.exp(sink_ref[hp] - m_scrs[hp][...])
        o = acc[0:HEAD_DIM, :] / den
        both = jnp.concatenate([o[:, :QT], o[:, QT:]], axis=0)
        o_ref[:, hp * LANE:(hp + 1) * LANE] = both.T.astype(o_ref.dtype)


def _attn_scratch(keys):
    return ([pltpu.VMEM((1, PW), F32) for _ in range(PAIRS)]
            + [pltpu.VMEM((VROWS, PW), F32) for _ in range(PAIRS)]
            + [pltpu.VMEM((keys, PW), F32) for _ in range(2 * LOCKSTEP)])


def _split_scratch(scr):
    return scr[:PAIRS], scr[PAIRS:2 * PAIRS], scr[2 * PAIRS:]


CMP_CHUNK = 128


def _cmp_select_kernel(q_ref, kc_ref, vct_ref, tab_ref, o_ref, sel_ref,
                       s_scr, p_scr, o_scr, imp_scr, *, nc, ns):
    n = pl.program_id(2)
    pad = CMP_BAND - 8
    cdt = kc_ref.dtype
    chunks = (8 * n + 8 + CMP_CHUNK - 1) // CMP_CHUNK
    band = pl.multiple_of(8 * n, 8)
    row = lax.broadcasted_iota(jnp.int32, (CMP_CHUNK, QT), 0)
    col = lax.broadcasted_iota(jnp.int32, (CMP_CHUNK, QT), 1)
    alive = (CMP_LEN - 1) <= QT * n + col[0:1]

    for h in range(HPG):
        s_scr[h, pl.ds(0, pad), :] = jnp.zeros((pad, QT), F32)

    def chunk_rows(j):
        return pl.ds(pl.multiple_of(j * CMP_CHUNK, CMP_CHUNK), CMP_CHUNK)

    def padded_rows(j):
        return pl.ds(pl.multiple_of(pad + j * CMP_CHUNK, 8), CMP_CHUNK)

    def logits(j, c):
        kc = kc_ref[chunk_rows(j), :]
        for h in range(HPG):
            s_scr[h, padded_rows(j), :] = lax.dot_general(kc, _head_queries(q_ref, h), _NT,
                                                          preferred_element_type=F32)
        return c

    lax.fori_loop(0, chunks, logits, 0)
    for h in range(HPG):
        s_scr[h, pl.ds(band, CMP_BAND), :] = s_scr[h, pl.ds(band, CMP_BAND), :] + tab_ref[h]

    def masked(j):
        valid = CMP_STRIDE * (j * CMP_CHUNK + row) + (CMP_LEN - 1) <= QT * n + col
        return [jnp.where(valid, s_scr[h, padded_rows(j), :], NEG) for h in range(HPG)]

    def head_max(j, ms):
        return tuple(jnp.maximum(m, jnp.max(s, axis=0, keepdims=True))
                     for m, s in zip(ms, masked(j)))

    ms = lax.fori_loop(0, chunks, head_max, tuple(jnp.full((1, QT), NEG, F32) for _ in range(HPG)))
    for h in range(HPG):
        o_scr[h] = jnp.zeros((HEAD_DIM, QT), F32)

    def values(j, ls):
        out = []
        for h, s in enumerate(masked(j)):
            p = jnp.exp(s - ms[h])
            p_scr[h, chunk_rows(j), :] = p
            o_scr[h] += jnp.dot(vct_ref[j], p.astype(cdt), preferred_element_type=F32)
            out.append(ls[h] + jnp.sum(p, axis=0, keepdims=True))
        return tuple(out)

    ls = lax.fori_loop(0, chunks, values, tuple(jnp.zeros((1, QT), F32) for _ in range(HPG)))
    invs = [jnp.where(alive, 1.0 / l, 0.0) for l in ls]
    for hp in range(PAIRS):
        both = jnp.concatenate([o_scr[2 * hp] * invs[2 * hp],
                                o_scr[2 * hp + 1] * invs[2 * hp + 1]], axis=0)
        o_ref[:, hp * LANE:(hp + 1) * LANE] = both.T.astype(o_ref.dtype)

    imp_scr[...] = jnp.zeros(imp_scr.shape, F32)

    def importance(j, c):
        tot = jnp.zeros((CMP_CHUNK, QT), F32)
        for h in range(HPG):
            tot = tot + p_scr[h, chunk_rows(j), :] * invs[h]
        imp_scr[pl.ds(pl.multiple_of(8 + j * CMP_CHUNK, 8), CMP_CHUNK), :] = tot
        return c

    lax.fori_loop(0, chunks, importance, 0)
    imp_s = jnp.zeros((ns, QT), F32)
    for r, w in zip((-1, 0, 1, 2, 3), (1.0, 2.0, 2.0, 2.0, 1.0)):
        imp_s = imp_s + w * imp_scr[pl.ds(8 + r, ns, stride=4), :]
    j = lax.broadcasted_iota(jnp.int32, (ns, QT), 0)
    qi = lax.broadcasted_iota(jnp.int32, (ns, QT), 1)
    cb = (QT // SLC_BLOCK) * n + qi // SLC_BLOCK
    forced = (j == 0) | (j == cb) | (j == cb - 1)
    score = jnp.where(j > cb, NEG, imp_s + jnp.where(forced, FORCE_BONUS, 0.0))
    _, _, picked = _topk_rows(score, min(N_SELECT, ns))
    sel_ref[...] = jnp.where(picked, 1.0, 0.0)


GW = HPG * HEAD_DIM


def _q_spec(nq, qcol):
    return pl.BlockSpec((QT, GW), lambda b, g, n: (b * nq + n, qcol + g))


def _o_spec(nq):
    return pl.BlockSpec((QT, GW), lambda b, g, n: (b * nq + n, g))


def cmp_select(proj, qcol, kc, vct, tab, *, out_dtype):
    B, G, nc = kc.shape[:3]
    H, Dh = HPG, HEAD_DIM
    S = nc * CMP_STRIDE
    ns, nq = S // SLC_BLOCK, S // QT
    kern = functools.partial(_cmp_select_kernel, nc=nc, ns=ns)
    return pl.pallas_call(
        kern,
        grid=(B, G, nq),
        in_specs=[_q_spec(nq, qcol),
                  pl.BlockSpec((None, None, nc, 2 * Dh), lambda b, g, n: (b, g, 0, 0)),
                  pl.BlockSpec((None, None, nc // CMP_CHUNK, Dh, CMP_CHUNK),
                               lambda b, g, n: (b, g, 0, 0, 0)),
                  pl.BlockSpec((None, H, CMP_BAND, QT), lambda b, g, n: (g, 0, 0, 0))],
        out_specs=[_o_spec(nq),
                   pl.BlockSpec((None, None, None, ns, QT), lambda b, g, n: (b, g, n, 0, 0))],
        out_shape=[jax.ShapeDtypeStruct((B * S, G * GW), out_dtype),
                   jax.ShapeDtypeStruct((B, G, nq, ns, QT), F32)],
        scratch_shapes=[pltpu.VMEM((H, CMP_BAND - 8 + nc, QT), F32),
                        pltpu.VMEM((H, nc, QT), F32),
                        pltpu.VMEM((H, Dh, QT), F32),
                        pltpu.VMEM((nc + 16, QT), F32)],
        compiler_params=_cparams(("parallel", "parallel", "arbitrary")),
        name="cmp_select",
    )(proj, kc, vct, tab)


SLC_TILE = 512
SLC_SUBS = SLC_TILE // KT


FAR_UNROLL = 4
NEAR_UNROLL = 2
TAB_ZERO = NEAR_TILES
TAB_FUTURE = NEAR_TILES + 1


def _slc_kernel(q_ref, k_ref, vt_ref, sel_ref, tab_ref, o_ref, *scr):
    m_scrs, acc_scrs, s_scrs = _split_scratch(scr)
    n = pl.program_id(2)
    for hp in range(PAIRS):
        m_scrs[hp][...] = jnp.full((1, PW), NEG, F32)
        acc_scrs[hp][...] = jnp.zeros((VROWS, PW), F32)

    def tile_mask(kt):
        s8 = jnp.tile(sel_ref[kt], (1, 2))
        rows = [jnp.broadcast_to(s8[b:b + 1], (SLC_BLOCK, PW))
                for b in range(SLC_TILE // SLC_BLOCK)]
        return jnp.concatenate(rows, axis=0) > 0.5

    def run_tiles(kts, near):
        masks = [tile_mask(kt) for kt in kts]
        ks = [k_ref[pl.ds(pl.multiple_of(kt * SLC_TILE, SLC_TILE), SLC_TILE), :] for kt in kts]
        vts = [[vt_ref[SLC_SUBS * kt + j] for j in range(SLC_SUBS)] for kt in kts]
        chains = [(i, hp) for i in range(len(kts)) for hp in range(PAIRS)]
        logits = lambda c: lax.dot_general(ks[c[0]], _pair_queries(q_ref, c[1]), _NT,
                                           preferred_element_type=F32)

        def masked(c, s):
            i, hp = c
            if near:
                offs = [n - (SLC_SUBS * kts[i] + j) for j in range(SLC_SUBS)]
                offs = [jnp.where(t < 0, TAB_FUTURE, jnp.minimum(t, TAB_ZERO)) for t in offs]
                s = s + jnp.concatenate([_pair_bias(tab_ref, hp, t) for t in offs], axis=0)
            return jnp.where(masks[i], s, NEG)

        _run_chains(chains, logits, masked, lambda c: vts[c[0]], m_scrs, acc_scrs, s_scrs,
                    k_ref.dtype)

    def far_body(i, carry):
        run_tiles([FAR_UNROLL * i + u for u in range(FAR_UNROLL)], False)
        return carry

    def far_rest_body(kt, carry):
        run_tiles([kt], False)
        return carry

    def near_body(i, carry):
        run_tiles([n_far + NEAR_UNROLL * i + u for u in range(NEAR_UNROLL)], True)
        return carry

    def near_rest_body(kt, carry):
        run_tiles([kt], True)
        return carry

    n_far = jnp.maximum((n - (NEAR_TILES - 1)) // SLC_SUBS, 0)
    n_trips = n_far // FAR_UNROLL
    lax.fori_loop(0, n_trips, far_body, 0)
    lax.fori_loop(n_trips * FAR_UNROLL, n_far, far_rest_body, 0)
    n_tiles = n // SLC_SUBS + 1
    near_trips = (n_tiles - n_far) // NEAR_UNROLL
    lax.fori_loop(0, near_trips, near_body, 0)
    lax.fori_loop(n_far + near_trips * NEAR_UNROLL, n_tiles, near_rest_body, 0)
    _write_heads(o_ref, acc_scrs)


def slc_attention(proj, qcol, k, vt, sel, tab, *, out_dtype):
    B, G, S = k.shape[:3]
    H, Dh = HPG, HEAD_DIM
    nq = S // QT
    blocks = SLC_TILE // SLC_BLOCK
    return pl.pallas_call(
        _slc_kernel,
        grid=(B, G, nq),
        in_specs=[_q_spec(nq, qcol),
                  pl.BlockSpec((None, None, S, 2 * Dh), lambda b, g, n: (b, g, 0, 0)),
                  pl.BlockSpec((None, None, S // KT, VROWS, KT), lambda b, g, n: (b, g, 0, 0, 0)),
                  pl.BlockSpec((None, None, None, S // SLC_TILE, blocks, QT),
                               lambda b, g, n: (b, g, n, 0, 0, 0)),
                  pl.BlockSpec((None, H, NEAR_TILES + 2, KT, QT), lambda b, g, n: (g, 0, 0, 0, 0))],
        out_specs=_o_spec(nq),
        out_shape=jax.ShapeDtypeStruct((B * S, G * GW), out_dtype),
        scratch_shapes=_attn_scratch(SLC_TILE),
        compiler_params=_cparams(("parallel", "parallel", "arbitrary")),
        name="slc_attention",
    )(proj, k, vt, sel, tab)


def _window_kernel(q_ref, k_ref, vt_ref, tab_ref, sink_ref, o_ref, *scr, n_tiles):
    m_scrs, acc_scrs, s_scrs = _split_scratch(scr)
    n = pl.program_id(2)
    rows = n_tiles * KT
    k = k_ref[pl.ds(pl.multiple_of(n * KT, KT), rows), :]
    vts = [vt_ref[n + j] for j in range(n_tiles)]
    real = lax.broadcasted_iota(jnp.int32, (rows, PW), 0) >= (rows - QT) - QT * n
    for hp in range(PAIRS):
        m_scrs[hp][...] = sink_ref[hp]
        acc_scrs[hp][...] = jnp.zeros((VROWS, PW), F32)

    def logits(c):
        return lax.dot_general(k, _pair_queries(q_ref, c[1]), _NT, preferred_element_type=F32)

    def masked(c, s):
        hp = c[1]
        s = s + jnp.concatenate([tab_ref[2 * hp], tab_ref[2 * hp + 1]], axis=1)
        return jnp.where(real, s, NEG)

    _run_chains([(0, hp) for hp in range(PAIRS)], logits, masked, lambda c: vts,
                m_scrs, acc_scrs, s_scrs, k_ref.dtype)
    _write_heads(o_ref, acc_scrs, sink_ref, m_scrs)


def window_attention(proj, qcol, k, vt, tab, sink_row, *, out_dtype):
    B, G = k.shape[:2]
    H, Dh = HPG, HEAD_DIM
    rows = tab.shape[2]
    S = k.shape[2] - (rows - QT)
    nq = S // QT
    kern = functools.partial(_window_kernel, n_tiles=rows // KT)
    return pl.pallas_call(
        kern,
        grid=(B, G, nq),
        in_specs=[_q_spec(nq, qcol),
                  pl.BlockSpec((None, None, k.shape[2], 2 * Dh), lambda b, g, n: (b, g, 0, 0)),
                  pl.BlockSpec((None, None, vt.shape[2], VROWS, KT), lambda b, g, n: (b, g, 0, 0, 0)),
                  pl.BlockSpec((None, H, rows, QT), lambda b, g, n: (g, 0, 0, 0)),
                  pl.BlockSpec((None, PAIRS, 1, PW), lambda b, g, n: (g, 0, 0, 0))],
        out_specs=_o_spec(nq),
        out_shape=jax.ShapeDtypeStruct((B * S, G * GW), out_dtype),
        scratch_shapes=_attn_scratch(rows),
        compiler_params=_cparams(("parallel", "parallel", "arbitrary")),
        name="window_attention",
    )(proj, k, vt, tab, sink_row)


def _split_hi_lo(x, cdt):
    hi = x.astype(cdt)
    lo = (x - hi.astype(F32)).astype(cdt)
    return hi, lo


def _out_proj_kernel(oc_ref, os_ref, ow_ref, ob_ref, gate_ref, ex_ref, wa_ref, wb_ref, h_ref,
                     o_ref, oa_scr):
    cdt = wa_ref.dtype

    @pl.when(pl.program_id(1) == 0)
    def _():
        g = jax.nn.sigmoid(gate_ref[...].astype(F32))
        g_hi, g_lo = _split_hi_lo(g, cdt)
        o_a = jnp.zeros(oa_scr.shape, F32)
        for c, br_ref in enumerate((oc_ref, os_ref, ow_ref)):
            ge = (jnp.dot(g_hi, ex_ref[c], preferred_element_type=F32)
                  + jnp.dot(g_lo, ex_ref[c], preferred_element_type=F32))
            o_a = o_a + ge * br_ref[...].astype(F32)
        oa_scr[...] = o_a.astype(cdt)

    o_ref[...] = (h_ref[...]
                  + jnp.dot(oa_scr[...], wa_ref[...], preferred_element_type=F32)
                  + jnp.dot(ob_ref[...], wb_ref[...], preferred_element_type=F32))


def out_proj(o_cmp, o_slc, o_win, o_b, gate, expand, w_a, w_b, h, *, tm, tn):
    T, WA = o_cmp.shape
    D = h.shape[1]
    row = lambda i, j: (i, 0)
    return pl.pallas_call(
        _out_proj_kernel,
        grid=(T // tm, D // tn),
        in_specs=[pl.BlockSpec((tm, WA), row), pl.BlockSpec((tm, WA), row),
                  pl.BlockSpec((tm, WA), row), pl.BlockSpec((tm, WA), row),
                  pl.BlockSpec((tm, LANE), row),
                  pl.BlockSpec((3, LANE, WA), lambda i, j: (0, 0, 0)),
                  pl.BlockSpec((WA, tn), lambda i, j: (0, j)),
                  pl.BlockSpec((WA, tn), lambda i, j: (0, j)),
                  pl.BlockSpec((tm, tn), lambda i, j: (i, j))],
        out_specs=pl.BlockSpec((tm, tn), lambda i, j: (i, j)),
        out_shape=jax.ShapeDtypeStruct((T, D), F32),
        scratch_shapes=[pltpu.VMEM((tm, WA), w_a.dtype)],
        compiler_params=_cparams(("parallel", "arbitrary")),
        name="out_proj",
    )(o_cmp, o_slc, o_win, o_b, gate, expand, w_a, w_b, h)


_CAND_NB = tuple(PEER_TOPK // (a + 1) for a in range(PEER_TOPK))
_CAND_START = tuple(int(v) for v in np.cumsum((0,) + _CAND_NB[:-1]))
_CAND_ROWS = sum(_CAND_NB)
_CAND_PAD = -(-_CAND_ROWS // 8) * 8
PEER_TOKEN_UNROLL = 16
PEER_HEADS_PER_TRIP = 4
SLAB_PITCH = LANE + 8


def _peer_route_kernel(q_ref, sk_ref, w_ref, cand_scr, e1t_scr, e2t_scr, gt_scr,
                       e1_scr, e2_scr, g_scr, slab_scr):
    cdt = q_ref.dtype
    tt = q_ref.shape[1]
    nk, k = PEER_NKEYS, PEER_TOPK
    for c in range(PEER_HEADS_PER_TRIP):
        cand_scr[c, pl.ds(_CAND_ROWS, _CAND_PAD - _CAND_ROWS), :] = jnp.full(
            (_CAND_PAD - _CAND_ROWS, tt), REMOVED, F32)

    def per_heads(trip, carry):
        heads = [PEER_HEADS_PER_TRIP * trip + c for c in range(PEER_HEADS_PER_TRIP)]
        scores = []
        for h in heads:
            scores.append(lax.dot_general(sk_ref[0], q_ref[2 * h], _NT,
                                          preferred_element_type=F32))
            scores.append(lax.dot_general(sk_ref[1], q_ref[2 * h + 1], _NT,
                                          preferred_element_type=F32))
        halves = _topk_rows_multi(scores, k)
        for c in range(PEER_HEADS_PER_TRIP):
            v1, v2 = halves[2 * c][0], halves[2 * c + 1][0]
            for a in range(k):
                cand_scr[c, pl.ds(_CAND_START[a], _CAND_NB[a]), :] = (
                    v1[a:a + 1] + v2[0:_CAND_NB[a]])
        best = _topk_rows_multi([cand_scr[c] for c in range(PEER_HEADS_PER_TRIP)], k)
        for c, h in enumerate(heads):
            i1, i2 = halves[2 * c][1], halves[2 * c + 1][1]
            sc, ci, _ = best[c]
            ai = jnp.zeros((k, tt), jnp.int32)
            bi = ci
            for a in range(1, k):
                later = ci >= _CAND_START[a]
                ai = ai + jnp.where(later, 1, 0)
                bi = bi - jnp.where(later, _CAND_NB[a - 1], 0)
            e1 = jnp.zeros((k, tt), jnp.int32)
            e2 = jnp.zeros((k, tt), jnp.int32)
            for a in range(k):
                e1 = e1 + jnp.where(ai == a, i1[a:a + 1], 0)
                e2 = e2 + jnp.where(bi == a, i2[a:a + 1], 0)
            ex = jnp.exp(sc - sc[0:1])
            rows = pl.ds(pl.multiple_of(h * k, k), k)
            e1t_scr[rows, :] = e1.astype(F32)
            e2t_scr[rows, :] = e2.astype(F32)
            gt_scr[rows, :] = ex / jnp.sum(ex, axis=0, keepdims=True)
        return carry

    lax.fori_loop(0, PEER_HEADS // PEER_HEADS_PER_TRIP, per_heads, 0)
    e1_scr[...] = e1t_scr[...].T
    e2_scr[...] = e2t_scr[...].T
    g_scr[...] = gt_scr[...].T

    key_id = lax.broadcasted_iota(jnp.int32, (nk, PEER_HEADS * k), 0).astype(F32)

    def per_token(t, carry):
        e1r = e1_scr[pl.ds(t, 1), :]
        e2r = e2_scr[pl.ds(t, 1), :]
        gr = g_scr[pl.ds(t, 1), :]
        a = jnp.where(key_id == e1r, gr, 0.0).astype(cdt)
        b = jnp.where(key_id == e2r, 1.0, 0.0).astype(cdt)
        slab_scr[pl.ds(t, nk, stride=SLAB_PITCH), :] = lax.dot_general(
            a, b, _NT, preferred_element_type=F32)
        return carry

    lax.fori_loop(0, tt, per_token, 0, unroll=PEER_TOKEN_UNROLL)
    for e1 in range(nk):
        w_ref[:, e1 * nk:(e1 + 1) * nk] = slab_scr[pl.ds(e1 * SLAB_PITCH, tt), :].astype(w_ref.dtype)


def peer_route(q, subkeys, *, tt):
    nq, T, half = q.shape
    nk = PEER_NKEYS
    slots = PEER_HEADS * PEER_TOPK
    return pl.pallas_call(
        _peer_route_kernel,
        grid=(T // tt,),
        in_specs=[pl.BlockSpec((nq, tt, half), lambda i: (0, i, 0)),
                  pl.BlockSpec((2, nk, half), lambda i: (0, 0, 0))],
        out_specs=pl.BlockSpec((tt, nk * nk), lambda i: (i, 0)),
        out_shape=jax.ShapeDtypeStruct((T, nk * nk), q.dtype),
        scratch_shapes=[pltpu.VMEM((PEER_HEADS_PER_TRIP, _CAND_PAD, tt), F32),
                        pltpu.VMEM((slots, tt), F32), pltpu.VMEM((slots, tt), F32),
                        pltpu.VMEM((slots, tt), F32),
                        pltpu.VMEM((tt, slots), F32), pltpu.VMEM((tt, slots), F32),
                        pltpu.VMEM((tt, slots), F32), pltpu.VMEM((nk * SLAB_PITCH, nk), F32)],
        compiler_params=_cparams(("parallel",)),
        name="peer_route",
    )(q, subkeys)


def _peer_dense_kernel(xn_ref, ut_ref, v_ref, w_ref, h_ref, o_ref):
    @pl.when(pl.program_id(1) == 0)
    def _():
        o_ref[...] = h_ref[...]
    hid = jnp.dot(xn_ref[...], ut_ref[...], preferred_element_type=F32)
    c = (w_ref[...].astype(F32) * jax.nn.gelu(hid)).astype(v_ref.dtype)
    o_ref[...] += jnp.dot(c, v_ref[...], preferred_element_type=F32)


def peer_dense(xn, ut, v, w, h, *, tt, te):
    T, D = xn.shape
    E = v.shape[0]
    return pl.pallas_call(
        _peer_dense_kernel,
        grid=(T // tt, E // te),
        in_specs=[pl.BlockSpec((tt, D), lambda i, e: (i, 0)),
                  pl.BlockSpec((D, te), lambda i, e: (0, e)),
                  pl.BlockSpec((te, D), lambda i, e: (e, 0)),
                  pl.BlockSpec((tt, te), lambda i, e: (i, e)),
                  pl.BlockSpec((tt, D), lambda i, e: (i, 0))],
        out_specs=pl.BlockSpec((tt, D), lambda i, e: (i, 0)),
        out_shape=jax.ShapeDtypeStruct((T, D), F32),
        compiler_params=_cparams(("parallel", "arbitrary")),
        name="peer_dense",
    )(xn, ut, v, w, h)


def _window_table(rel_bias_heads, window):
    keys = window + QT
    r = np.arange(keys)[:, None]
    i = np.arange(QT)[None, :]
    dist = window + i - r
    span = keys + QT - 1
    y = rel_bias_heads.astype(F32)[jnp.asarray(_rel_bucket_np(window + QT - 1 - np.arange(span)))].T
    skew = _toeplitz(y, QT, span)
    tab = jnp.transpose(skew[:, :, QT - 1:QT - 1 + keys], (0, 2, 1))
    return jnp.where(jnp.asarray((dist >= 0) & (dist < window))[None], tab, NEG)


def _front_pad(a, rows):
    return jnp.pad(a, ((0, 0), (rows, 0)) + ((0, 0),) * (a.ndim - 2))


def _value_tiles(v, cdt):
    B, S, G, Dh = v.shape
    vt = jnp.transpose(v.reshape(B, S // KT, KT, G, Dh), (0, 3, 1, 4, 2))
    ones = jnp.ones(vt.shape[:3] + (ONES_ROWS, KT), vt.dtype)
    return jnp.concatenate([vt, ones], axis=3).astype(cdt)


def _tile(n, pref):
    return pref if n % pref == 0 else n


def _forward(x, attn_norm, w_in, cmp_pos_k, cmp_w1_k, cmp_w2_k, cmp_pos_v, cmp_w1_v, cmp_w2_v,
             sinks, w_out, ffn_norm, peer_wq, peer_subkeys, peer_u, peer_v, rel_bias, final_norm,
             cdt):
    B, S, D = x.shape
    T = B * S
    depth = w_in.shape[0]
    G, H, Dh = KV_GROUPS, HPG, HEAD_DIM
    QW = G * H * Dh
    KW = G * Dh
    NGATE = G * H * 3
    nsa_heads = G * H

    o_qa, o_kv, o_gate = 0, QW, QW + 6 * KW
    o_qb = o_gate + NGATE
    o_kvb = o_qb + QW
    proj_w = 2 * QW + 6 * KW + 2 * KW + LANE
    c_qb, c_kv, c_kvb, c_gate = QW, 2 * QW, 2 * QW + 6 * KW, 2 * QW + 8 * KW

    tok_tab, cmp_tab = _bias_tables(rel_bias)
    causal0 = jnp.asarray(np.arange(QT)[None, :] - np.arange(KT)[:, None] >= 0)
    slc_tab = tok_tab[:nsa_heads]
    slc_tab = jnp.concatenate(
        [jnp.where(causal0[None, None], slc_tab[:, :1], NEG), slc_tab[:, 1:],
         jnp.zeros((nsa_heads, 1, KT, QT), F32), jnp.full((nsa_heads, 1, KT, QT), NEG, F32)],
        axis=1).reshape(G, H, NEAR_TILES + 2, KT, QT)
    cmp_tab = cmp_tab.reshape(G, H, CMP_BAND, QT)
    win_tab = _window_table(rel_bias[:, :nsa_heads], NSA_WINDOW).reshape(
        G, H, NSA_WINDOW + QT, QT)
    swa_tab = _window_table(rel_bias[:, nsa_heads:], SWA_WINDOW).reshape(
        G, H, SWA_WINDOW + QT, QT)
    no_sink = jnp.full((G, PAIRS, 1, PW), NEG, F32)
    q_scale = Dh ** -0.5

    ex = np.zeros((3, LANE, QW), np.float32)
    for gh in range(G * H):
        for c in range(3):
            ex[c, gh * 3 + c, gh * Dh:(gh + 1) * Dh] = 1.0
    expand = jnp.asarray(ex, cdt)

    tm = _tile(T, 512)
    h = x.reshape(T, D)
    for l in range(depth):
        wi = w_in[l]
        w_re = jnp.concatenate(
            [wi[:, o_qa:o_qa + QW] * q_scale, wi[:, o_qb:o_qb + QW] * q_scale,
             wi[:, o_kv:o_kv + 6 * KW], wi[:, o_kvb:o_kvb + 2 * KW],
             wi[:, o_gate:o_gate + NGATE],
             jnp.zeros((D, LANE - NGATE), wi.dtype)], axis=1).astype(cdt)
        proj, = rms_matmul(h, attn_norm[l], w_re, tm=tm, tn=_tile(proj_w, 640), out_dtype=cdt,
                           with_xn=False)

        qcol_a, qcol_b = 0, c_qb // GW
        kv = proj[:, c_kv:c_kv + 6 * KW].reshape(B, S, 6, G, Dh)
        kvb = proj[:, c_kvb:c_kvb + 2 * KW].reshape(B, S, 2, G, Dh)
        gate = proj[:, c_gate:c_gate + LANE]

        kc_c = compress(kv[:, :, 0].reshape(B, S, KW), cmp_pos_k[l], cmp_w1_k[l], cmp_w2_k[l], cdt)
        vc_c = compress(kv[:, :, 1].reshape(B, S, KW), cmp_pos_v[l], cmp_w1_v[l], cmp_w2_v[l], cdt)
        nc = S // CMP_STRIDE
        group_keys = lambda a: jnp.tile(jnp.transpose(a, (0, 2, 1, 3)), (1, 1, 1, 2))
        vc_t = jnp.transpose(vc_c.reshape(B, nc // CMP_CHUNK, CMP_CHUNK, G, Dh),
                             (0, 3, 1, 4, 2))
        o_cmp, sel = cmp_select(proj, qcol_a, group_keys(kc_c.reshape(B, nc, G, Dh)), vc_t,
                                cmp_tab, out_dtype=cdt)
        sel = sel.reshape(B, G, S // QT, S // SLC_TILE, SLC_TILE // SLC_BLOCK, QT)

        o_slc = slc_attention(proj, qcol_a, group_keys(kv[:, :, 2]),
                              _value_tiles(kv[:, :, 3], cdt), sel, slc_tab, out_dtype=cdt)
        o_win = window_attention(proj, qcol_a, group_keys(_front_pad(kv[:, :, 4], NSA_WINDOW)),
                                 _value_tiles(_front_pad(kv[:, :, 5], NSA_WINDOW), cdt),
                                 win_tab, no_sink, out_dtype=cdt)
        sink_row = jnp.repeat(sinks[l].astype(F32).reshape(G, H), QT, axis=1).reshape(G, PAIRS, 1, PW)
        o_b = window_attention(proj, qcol_b, group_keys(_front_pad(kvb[:, :, 0], SWA_WINDOW)),
                               _value_tiles(_front_pad(kvb[:, :, 1], SWA_WINDOW), cdt),
                               swa_tab, sink_row, out_dtype=cdt)

        wo = w_out[l].astype(cdt)
        h = out_proj(o_cmp, o_slc, o_win, o_b,
                     gate, expand, wo[:QW], wo[QW:], h, tm=tm, tn=_tile(D, 2048))

        pq, xn = rms_matmul(h, ffn_norm[l], peer_wq[l].astype(cdt), tm=tm,
                            tn=_tile(peer_wq.shape[2], 1024), out_dtype=cdt, with_xn=True)
        pq_halves = jnp.transpose(pq.reshape(T, 2 * PEER_HEADS, -1), (1, 0, 2))
        w_route = peer_route(pq_halves, peer_subkeys[l].astype(cdt), tt=LANE)
        h = peer_dense(xn, peer_u[l].T.astype(cdt), peer_v[l].astype(cdt), w_route, h,
                       tt=_tile(T, 1024), te=512)
    return rmsnorm_final(h, final_norm, tm=tm).reshape(B, S, D)


def kernel(x, attn_norm, w_in, cmp_pos_k, cmp_w1_k, cmp_w2_k, cmp_pos_v, cmp_w1_v, cmp_w2_v,
           sinks, w_out, ffn_norm, peer_wq, peer_subkeys, peer_u, peer_v, rel_bias, final_norm):
    return _forward(x, attn_norm, w_in, cmp_pos_k, cmp_w1_k, cmp_w2_k, cmp_pos_v, cmp_w1_v,
                    cmp_w2_v, sinks, w_out, ffn_norm, peer_wq, peer_subkeys, peer_u, peer_v,
                    rel_bias, final_norm, BF16)
```
